```python
import jax, jax.numpy as jnp
from jax import lax
import numpy as np

D_MODEL = 2048
BATCH = 4
SEQ = 4096
DEPTH = 4

GRID_W = 64
CTX_LEN = 256
N_MIXERS = 2
N_RET_LAYERS = (DEPTH + 1) // 2
N_NA_LAYERS = DEPTH // 2
N_MOD = 9
D_FF = ((8 * D_MODEL // 3 + 127) // 128) * 128
RET_HEADS = 8
RET_DK = D_MODEL // RET_HEADS
RET_DV = 2 * RET_DK
RET_V_WIDTH = RET_HEADS * RET_DV
RET_CHUNK = 128
NA_HEADS = 16
NA_DH = D_MODEL // NA_HEADS
NA_KR = 8
NA_KC = 16
ROPE_BASE = 10000.0
EPS = 1e-6

kernel_name = 'hybrid_retention_natten_macaron_dit'


def _rmsnorm(x, g):
    xf = x.astype(jnp.float32)
    y = xf * lax.rsqrt(jnp.mean(xf * xf, axis=-1, keepdims=True) + EPS)
    return (y * g.astype(jnp.float32)).astype(x.dtype)


def _ada_norm(x, g, shift, scale):
    return _rmsnorm(x, g) * (1 + scale) + shift


def _swiglu(h, w_gu, w_down):
    gate, up = jnp.split(h @ w_gu, 2, axis=-1)
    return (jax.nn.silu(gate) * up) @ w_down


def _heads(t, n_heads):
    b, n, w = t.shape
    return t.reshape(b, n, n_heads, w // n_heads).transpose(0, 2, 1, 3)


def _axial_rope_angles(n_tok, dim):
    t = jnp.arange(n_tok)
    row = (t // GRID_W).astype(jnp.float32)
    col = (t % GRID_W).astype(jnp.float32)
    half = dim // 2
    freqs = ROPE_BASE ** (-jnp.arange(0, half, 2, dtype=jnp.float32) / half)
    return row[:, None] * freqs, col[:, None] * freqs


def _rope_half(x, ang):
    x1, x2 = jnp.split(x, 2, axis=-1)
    cos, sin = jnp.cos(ang), jnp.sin(ang)
    return jnp.concatenate([x1 * cos - x2 * sin, x1 * sin + x2 * cos], axis=-1)


def _apply_axial_rope(x, ang_r, ang_c):
    xr, xc = jnp.split(x, 2, axis=-1)
    return jnp.concatenate([_rope_half(xr, ang_r), _rope_half(xc, ang_c)], axis=-1)


def _retention_chunks(q, k, v, log_gamma, state0):
    b, h, n, _ = q.shape
    dv = v.shape[-1]
    nc = n // RET_CHUNK
    pos = jnp.arange(RET_CHUNK, dtype=jnp.float32)
    lg = log_gamma[:, None]
    diff = pos[:, None] - pos[None, :]
    intra = jnp.where(diff[None] >= 0, jnp.exp(jnp.maximum(diff, 0.0)[None] * lg[:, :, None]), 0.0)
    q_decay = jnp.exp((pos + 1.0) * lg)[None, :, :, None]
    k_decay = jnp.exp((RET_CHUNK - 1.0 - pos) * lg)[None, :, :, None]
    c_decay = jnp.exp(RET_CHUNK * lg)[None, :, :, None]

    def to_chunks(t):
        return t.reshape(b, h, nc, RET_CHUNK, t.shape[-1]).transpose(2, 0, 1, 3, 4)

    def step(state, inp):
        qc, kc, vc = inp
        s = jnp.einsum('bhid,bhjd->bhij', qc, kc) * intra[None]
        o = jnp.einsum('bhij,bhjv->bhiv', s, vc) + jnp.einsum('bhid,bhdv->bhiv', qc, state) * q_decay
        state = state * c_decay + jnp.einsum('bhjd,bhjv->bhdv', kc * k_decay, vc)
        return state, o

    _, outs = lax.scan(step, state0, (to_chunks(q), to_chunks(k), to_chunks(v)))
    return outs.transpose(1, 2, 0, 3, 4).reshape(b, h, n, dv)


def _context_state(k, v, log_gamma, reverse):
    l = k.shape[2]
    pos = jnp.arange(l, dtype=jnp.float32)
    expo = pos if reverse else (l - 1.0 - pos)
    w = jnp.exp(expo[None, :] * log_gamma[:, None])
    return jnp.einsum('bhld,bhlv->bhdv', k * w[None, :, :, None], v)


def _retention_out(o, g, w_out, dtype):
    mu = jnp.mean(o, axis=-1, keepdims=True)
    var = jnp.mean(jnp.square(o - mu), axis=-1, keepdims=True)
    on = (o - mu) * lax.rsqrt(var + EPS)
    b, h, n, dv = on.shape
    on = on.transpose(0, 2, 1, 3).reshape(b, n, h * dv).astype(dtype)
    return (on * jax.nn.silu(g)) @ w_out


def _retention_mixer(h_lat, h_ctx, w_in, w_out, log_decay, ang_r, ang_c, need_ctx):
    lg_f = -jnp.exp(log_decay[0].astype(jnp.float32))
    lg_b = -jnp.exp(log_decay[1].astype(jnp.float32))

    def project(h):
        q, k, v, g = jnp.split(h @ w_in, [D_MODEL, 2 * D_MODEL, 2 * D_MODEL + RET_V_WIDTH], axis=-1)
        q = _heads(q, RET_HEADS).astype(jnp.float32)
        k = _heads(k, RET_HEADS).astype(jnp.float32) * (RET_DK ** -0.5)
        v = _heads(v, RET_HEADS).astype(jnp.float32)
        return q, k, v, g

    flip = lambda t: jnp.flip(t, axis=2)
    q, k, v, g = project(h_lat)
    q = _apply_axial_rope(q, ang_r, ang_c)
    k = _apply_axial_rope(k, ang_r, ang_c)
    qc, kc, vc, gc = project(h_ctx)
    state_f = _context_state(kc, vc, lg_f, False)
    state_b = _context_state(kc, vc, lg_b, True)
    o = _retention_chunks(q, k, v, lg_f, state_f) + flip(_retention_chunks(flip(q), flip(k), flip(v), lg_b, state_b))
    y_lat = _retention_out(o, g, w_out, h_lat.dtype)
    y_ctx = None
    if need_ctx:
        zero = jnp.zeros_like(state_f)
        oc = _retention_chunks(qc, kc, vc, lg_f, zero) + flip(_retention_chunks(flip(qc), flip(kc), flip(vc), lg_b, zero))
        y_ctx = _retention_out(oc, gc, w_out, h_ctx.dtype)
    return y_lat, y_ctx


def _na_mixer(h_lat, h_ctx, w_in, w_out, rpb, need_ctx):
    b, n, _ = h_lat.shape
    rows = n // GRID_W
    kr = min(NA_KR, rows)
    q, k, v = jnp.split(h_lat @ w_in, 3, axis=-1)
    q = _heads(q, NA_HEADS) * (NA_DH ** -0.5)
    k = _heads(k, NA_HEADS)
    v = _heads(v, NA_HEADS)
    qc, kc, vc = jnp.split(h_ctx @ w_in, 3, axis=-1)
    qc = _heads(qc, NA_HEADS) * (NA_DH ** -0.5)
    kc = _heads(kc, NA_HEADS)
    vc = _heads(vc, NA_HEADS)
    grid = lambda t: t.reshape(b, NA_HEADS, rows, GRID_W, NA_DH)
    q_g, k_g, v_g = grid(q), grid(k), grid(v)
    col = jnp.arange(GRID_W)
    cs = jnp.clip(col - NA_KC // 2, 0, GRID_W - NA_KC)
    col_idx = cs[:, None] + jnp.arange(NA_KC)[None, :]
    dc = col_idx - col[:, None] + (NA_KC - 1)

    def row_block(r):
        rs = jnp.clip(r - kr // 2, 0, rows - kr)
        qb = lax.dynamic_index_in_dim(q_g, r, axis=2, keepdims=False)
        kb = lax.dynamic_slice_in_dim(k_g, rs, kr, axis=2)
        vb = lax.dynamic_slice_in_dim(v_g, rs, kr, axis=2)
        kw = kb[:, :, :, col_idx]
        vw = vb[:, :, :, col_idx]
        dr = rs + jnp.arange(kr) - r + (NA_KR - 1)
        bias = rpb[:, dr[:, None, None], dc[None, :, :]].transpose(0, 2, 1, 3)
        s_loc = jnp.einsum('bhqd,bhrqkd->bhqrk', qb, kw).astype(jnp.float32) + bias[None].astype(jnp.float32)
        s_ctx = jnp.einsum('bhqd,bhcd->bhqc', qb, kc).astype(jnp.float32)
        s = jnp.concatenate([s_loc.reshape(b, NA_HEADS, GRID_W, kr * NA_KC), s_ctx], axis=-1)
        p = jax.nn.softmax(s, axis=-1)
        p_loc = p[..., :kr * NA_KC].reshape(b, NA_HEADS, GRID_W, kr, NA_KC)
        p_ctx = p[..., kr * NA_KC:]
        o = jnp.einsum('bhqrk,bhrqkd->bhqd', p_loc, vw) + jnp.einsum('bhqc,bhcd->bhqd', p_ctx, vc)
        return o.astype(h_lat.dtype)

    o = lax.map(row_block, jnp.arange(rows))
    o = o.transpose(1, 0, 3, 2, 4).reshape(b, n, NA_HEADS * NA_DH)
    y_lat = o @ w_out
    y_ctx = None
    if need_ctx:
        pc = jax.nn.softmax(jnp.einsum('bhqd,bhcd->bhqc', qc, kc).astype(jnp.float32), axis=-1)
        oc = jnp.einsum('bhqc,bhcd->bhqd', pc, vc).astype(h_ctx.dtype)
        bc, _, lc, _ = oc.shape
        y_ctx = oc.transpose(0, 2, 1, 3).reshape(bc, lc, NA_HEADS * NA_DH) @ w_out
    return y_lat, y_ctx


def setup_inputs(seed: int = 0) -> dict:
    key = jax.random.key(seed)
    ks = jax.random.split(key, 16)
    f32 = jnp.float32
    nrm = lambda k, shape, s: jax.random.normal(k, shape, f32) * s
    base_decay = jnp.log(-jnp.log1p(-(2.0 ** (-5.0 - jnp.arange(RET_HEADS, dtype=f32)))))
    return {
        'x': nrm(ks[0], (BATCH, SEQ, D_MODEL), 1.0),
        'c': nrm(ks[1], (BATCH, D_MODEL), 1.0),
        'ctx': nrm(ks[2], (BATCH, CTX_LEN, D_MODEL), 1.0),
        'c_ctx': nrm(ks[3], (D_MODEL,), 1.0),
        'ada_w': nrm(ks[4], (DEPTH, D_MODEL, N_MOD * D_MODEL), 0.5 * D_MODEL ** -0.5),
        'ada_b': nrm(ks[5], (DEPTH, N_MOD * D_MODEL), 0.01),
        'norm_g': 1.0 + nrm(ks[6], (DEPTH, 3, D_MODEL), 0.02),
        'ffn_w_gu': nrm(ks[7], (DEPTH, 2, D_MODEL, 2 * D_FF), D_MODEL ** -0.5),
        'ffn_w_down': nrm(ks[8], (DEPTH, 2, D_FF, D_MODEL), D_FF ** -0.5),
        'ret_w_in': nrm(ks[9], (N_RET_LAYERS, D_MODEL, 2 * D_MODEL + 2 * RET_V_WIDTH), D_MODEL ** -0.5),
        'ret_w_out': nrm(ks[10], (N_RET_LAYERS, RET_V_WIDTH, D_MODEL), RET_V_WIDTH ** -0.5),
        'ret_log_decay': base_decay[None, None, :] + nrm(ks[11], (N_RET_LAYERS, 2, RET_HEADS), 0.05),
        'na_w_in': nrm(ks[12], (N_NA_LAYERS, D_MODEL, 3 * D_MODEL), D_MODEL ** -0.5),
        'na_w_out': nrm(ks[13], (N_NA_LAYERS, D_MODEL, D_MODEL), D_MODEL ** -0.5),
        'na_rpb': nrm(ks[14], (N_NA_LAYERS, NA_HEADS, 2 * NA_KR - 1, 2 * NA_KC - 1), 0.1),
        'final_g': 1.0 + nrm(ks[15], (D_MODEL,), 0.02),
    }


def reference(x, c, ctx, c_ctx, ada_w, ada_b, norm_g, ffn_w_gu, ffn_w_down, ret_w_in, ret_w_out,
              ret_log_decay, na_w_in, na_w_out, na_rpb, final_g):
    n_lat = x.shape[1]
    ang_r, ang_c = _axial_rope_angles(n_lat, RET_DK)
    silu_c = jax.nn.silu(c)
    silu_cc = jax.nn.silu(c_ctx)
    for i in range(DEPTH):
        last = i == DEPTH - 1
        mod = (silu_c @ ada_w[i] + ada_b[i]).reshape(c.shape[0], N_MOD, D_MODEL)
        m = [mod[:, j][:, None, :] for j in range(N_MOD)]
        mod_c = (silu_cc @ ada_w[i] + ada_b[i]).reshape(N_MOD, D_MODEL)
        mc = [mod_c[j][None, None, :] for j in range(N_MOD)]
        x = x + 0.5 * m[2] * _swiglu(_ada_norm(x, norm_g[i, 0], m[0], m[1]), ffn_w_gu[i, 0], ffn_w_down[i, 0])
        ctx = ctx + 0.5 * mc[2] * _swiglu(_ada_norm(ctx, norm_g[i, 0], mc[0], mc[1]), ffn_w_gu[i, 0], ffn_w_down[i, 0])
        hx = _ada_norm(x, norm_g[i, 1], m[3], m[4])
        hc = _ada_norm(ctx, norm_g[i, 1], mc[3], mc[4])
        j = i // N_MIXERS
        if i % N_MIXERS == 0:
            yx, yc = _retention_mixer(hx, hc, ret_w_in[j], ret_w_out[j], ret_log_decay[j], ang_r, ang_c, not last)
        else:
            yx, yc = _na_mixer(hx, hc, na_w_in[j], na_w_out[j], na_rpb[j], not last)
        x = x + m[5] * yx
        x = x + 0.5 * m[8] * _swiglu(_ada_norm(x, norm_g[i, 2], m[6], m[7]), ffn_w_gu[i, 1], ffn_w_down[i, 1])
        if not last:
            ctx = ctx + mc[5] * yc
            ctx = ctx + 0.5 * mc[8] * _swiglu(_ada_norm(ctx, norm_g[i, 2], mc[6], mc[7]), ffn_w_gu[i, 1], ffn_w_down[i, 1])
    return _rmsnorm(x, final_g)
```

```python
import functools
import math

import numpy as np
import jax
import jax.numpy as jnp
from jax import lax
from jax.experimental import pallas as pl
from jax.experimental.pallas import tpu as pltpu

F32 = jnp.float32
BF16 = jnp.bfloat16

GRID_W = 64
N_MOD = 9
RET_HEADS = 8
RET_CHUNK = 128
NA_HEADS = 16
NA_KR = 8
NA_KC = 16
ROPE_BASE = 10000.0
EPS = 1e-6

V7X_VMEM_BYTES = 64 * 1024 * 1024
VMEM_LIMIT_BYTES = V7X_VMEM_BYTES - 8 * 1024 * 1024
MXU_DIM = 256
NEG_BIG = -1e30

FFN_TM = 512
FFN_FC = 512
PROJ_TM = 1024
PROJ_TN = 1024
OUT_TN = 512
MOD_TN = 1024
NA_QROWS = 4


def _cparams(*sem):
    return pltpu.CompilerParams(dimension_semantics=sem, vmem_limit_bytes=VMEM_LIMIT_BYTES)


def _silu(x):
    return x * (1.0 / (1.0 + jnp.exp(-x)))


def _ada_norm_tile(x, g, shift, scale):
    y = x * lax.rsqrt(jnp.mean(x * x, axis=-1, keepdims=True) + EPS) * g
    return y * (1.0 + scale) + shift


def _mod_kernel(c_ref, w_ref, b_ref, o_ref):
    s = _silu(c_ref[...]).astype(BF16)
    o_ref[...] = jnp.dot(s, w_ref[...].astype(BF16), preferred_element_type=F32) + b_ref[...]


def _modulation(cc, ada_w, ada_b):
    depth, d, n = ada_w.shape
    return pl.pallas_call(
        _mod_kernel,
        grid=(depth, n // MOD_TN),
        in_specs=[
            pl.BlockSpec((8, d), lambda l, j: (0, 0)),
            pl.BlockSpec((None, d, MOD_TN), lambda l, j: (l, 0, j)),
            pl.BlockSpec((None, 1, MOD_TN), lambda l, j: (l, 0, j)),
        ],
        out_specs=pl.BlockSpec((None, 8, MOD_TN), lambda l, j: (l, 0, j)),
        out_shape=jax.ShapeDtypeStruct((depth, 8, n), F32),
        compiler_params=_cparams("arbitrary", "arbitrary"),
    )(cc, ada_w, ada_b.reshape(depth, 1, n))


def _mod_row_map(layer, rows_per_mod, mod_base, tm):
    tiles_per_mod = rows_per_mod // tm
    return lambda t, j: (layer, mod_base + t // tiles_per_mod, 0, 0)


def _ffn_kernel(*refs, mod_off, g_row, fc, final):
    if final:
        x_ref, mod_ref, g_ref, wgu_ref, wd_ref, fg_ref, o_ref, h_ref = refs
    else:
        x_ref, mod_ref, g_ref, wgu_ref, wd_ref, o_ref, h_ref = refs
    j = pl.program_id(1)

    @pl.when(j == 0)
    def _():
        h = _ada_norm_tile(x_ref[...], g_ref[g_row:g_row + 1, :],
                           mod_ref[mod_off:mod_off + 1, :], mod_ref[mod_off + 1:mod_off + 2, :])
        h_ref[...] = h.astype(BF16)
        o_ref[...] = jnp.zeros_like(o_ref)

    gu = jnp.dot(h_ref[...], wgu_ref[...], preferred_element_type=F32)
    a = _silu(gu[:, :fc]) * gu[:, fc:]
    o_ref[...] += jnp.dot(a.astype(BF16), wd_ref[...], preferred_element_type=F32)

    @pl.when(j == pl.num_programs(1) - 1)
    def _():
        o = x_ref[...] + (0.5 * mod_ref[mod_off + 2:mod_off + 3, :]) * o_ref[...]
        if final:
            o = o * lax.rsqrt(jnp.mean(o * o, axis=-1, keepdims=True) + EPS) * fg_ref[...]
        o_ref[...] = o


def _ffn(x, mod, norm_g, wgu_c, wd_p, *, layer, which, rows_per_mod, mod_base, final_g=None):
    t, d = x.shape
    nfc, _, fc2 = wgu_c.shape
    fc = fc2 // 2
    tm = min(FFN_TM, t)
    final = final_g is not None
    in_specs = [
        pl.BlockSpec((tm, d), lambda i, j: (i, 0)),
        pl.BlockSpec((None, None, N_MOD, d), _mod_row_map(layer, rows_per_mod, mod_base, tm)),
        pl.BlockSpec((None, 3, d), lambda i, j: (layer, 0, 0)),
        pl.BlockSpec((None, d, fc2), lambda i, j: (j, 0, 0)),
        pl.BlockSpec((fc, d), lambda i, j: (j, 0)),
    ]
    args = [x, mod, norm_g, wgu_c, wd_p]
    if final:
        in_specs.append(pl.BlockSpec((1, d), lambda i, j: (0, 0)))
        args.append(final_g.reshape(1, d))
    return pl.pallas_call(
        functools.partial(_ffn_kernel, mod_off=6 * which, g_row=2 * which, fc=fc, final=final),
        grid=(t // tm, nfc),
        in_specs=in_specs,
        out_specs=pl.BlockSpec((tm, d), lambda i, j: (i, 0)),
        out_shape=jax.ShapeDtypeStruct((t, d), F32),
        scratch_shapes=[pltpu.VMEM((tm, d), BF16)],
        compiler_params=_cparams("parallel", "arbitrary"),
    )(*args)


def _prep_ffn_weights(w_gu, w_down):
    d, f2 = w_gu.shape
    f = f2 // 2
    nfc = -(-f // FFN_FC)
    pad = nfc * FFN_FC - f
    wg = jnp.pad(w_gu[:, :f].astype(BF16), ((0, 0), (0, pad))).reshape(d, nfc, FFN_FC)
    wu = jnp.pad(w_gu[:, f:].astype(BF16), ((0, 0), (0, pad))).reshape(d, nfc, FFN_FC)
    wgu_c = jnp.concatenate([wg, wu], axis=-1).transpose(1, 0, 2)
    wd_p = jnp.pad(w_down.astype(BF16), ((0, pad), (0, 0)))
    return wgu_c, wd_p


def _inproj_kernel(x_ref, mod_ref, g_ref, w_ref, cs_ref, o_ref, h_ref):
    @pl.when(pl.program_id(1) == 0)
    def _():
        h = _ada_norm_tile(x_ref[...], g_ref[1:2, :], mod_ref[3:4, :], mod_ref[4:5, :])
        h_ref[...] = h.astype(BF16)

    acc = jnp.dot(h_ref[...], w_ref[...], preferred_element_type=F32)
    o_ref[...] = (acc * cs_ref[...]).astype(o_ref.dtype)


def _inproj(x, mod, norm_g, w, col_scale, *, layer, rows_per_mod, mod_base):
    t, d = x.shape
    n = w.shape[1]
    tm = min(PROJ_TM, t)
    return pl.pallas_call(
        _inproj_kernel,
        grid=(t // tm, n // PROJ_TN),
        in_specs=[
            pl.BlockSpec((tm, d), lambda i, j: (i, 0)),
            pl.BlockSpec((None, None, N_MOD, d), _mod_row_map(layer, rows_per_mod, mod_base, tm)),
            pl.BlockSpec((None, 3, d), lambda i, j: (layer, 0, 0)),
            pl.BlockSpec((d, PROJ_TN), lambda i, j: (0, j)),
            pl.BlockSpec((1, PROJ_TN), lambda i, j: (0, j)),
        ],
        out_specs=pl.BlockSpec((tm, PROJ_TN), lambda i, j: (i, j)),
        out_shape=jax.ShapeDtypeStruct((t, n), BF16),
        scratch_shapes=[pltpu.VMEM((tm, d), BF16)],
        compiler_params=_cparams("parallel", "arbitrary"),
    )(x, mod, norm_g, w, col_scale)


def _outproj_kernel(y_ref, w_ref, x_ref, mod_ref, o_ref):
    acc = jnp.dot(y_ref[...], w_ref[...], preferred_element_type=F32)
    o_ref[...] = x_ref[...] + mod_ref[5:6, :] * acc


def _outproj(y, w, x, mod, *, layer, rows_per_mod, mod_base):
    t, d = x.shape
    kd = y.shape[1]
    tm = min(PROJ_TM, t)
    tiles_per_mod = rows_per_mod // tm
    return pl.pallas_call(
        _outproj_kernel,
        grid=(t // tm, d // OUT_TN),
        in_specs=[
            pl.BlockSpec((tm, kd), lambda i, j: (i, 0)),
            pl.BlockSpec((kd, OUT_TN), lambda i, j: (0, j)),
            pl.BlockSpec((tm, OUT_TN), lambda i, j: (i, j)),
            pl.BlockSpec((None, None, N_MOD, OUT_TN), lambda i, j: (layer, mod_base + i // tiles_per_mod, 0, j)),
        ],
        out_specs=pl.BlockSpec((tm, OUT_TN), lambda i, j: (i, j)),
        out_shape=jax.ShapeDtypeStruct((t, d), F32),
        compiler_params=_cparams("parallel", "arbitrary"),
    )(y, w, x, mod)


def _dot_nt(a, b):
    return lax.dot_general(a, b, (((1,), (1,)), ((), ())), preferred_element_type=F32)


def _dot_tn(a, b):
    return lax.dot_general(a, b, (((0,), (0,)), ((), ())), preferred_element_type=F32)


def _ret_consts(ld_ref):
    c = RET_CHUNK
    lg_f = -jnp.exp(ld_ref[0])
    lg_b = -jnp.exp(ld_ref[1])
    row = lax.broadcasted_iota(jnp.int32, (c, c), 0)
    col = lax.broadcasted_iota(jnp.int32, (c, c), 1)
    diff = (row - col).astype(F32)
    m_f = jnp.where(diff >= 0, jnp.exp(jnp.maximum(diff, 0.0) * lg_f), 0.0)
    m_b = jnp.where(diff <= 0, jnp.exp(jnp.maximum(-diff, 0.0) * lg_b), 0.0)
    pos = row.astype(F32)

    def col_vec(e):
        return e[:, :1]

    return dict(
        intra=m_f + m_b,
        qd_f=col_vec(jnp.exp((pos + 1.0) * lg_f)),
        qd_b=col_vec(jnp.exp((c - pos) * lg_b)),
        kd_f=col_vec(jnp.exp((c - 1.0 - pos) * lg_f)),
        kd_b=col_vec(jnp.exp(pos * lg_b)),
        cd_f=jnp.exp(float(c) * lg_f)[:, :1],
        cd_b=jnp.exp(float(c) * lg_b)[:, :1],
        lg_f=lg_f, lg_b=lg_b,
    )


def _ret_chunk_out(q, k, v, intra, inter):
    s = _dot_nt(q.astype(BF16), k.astype(BF16)) * intra
    o = jnp.dot(s.astype(BF16), v, preferred_element_type=F32)
    for state_bf16, qd in inter:
        o = o + jnp.dot(q.astype(BF16), state_bf16, preferred_element_type=F32) * qd
    return o


def _ret_gate(o, g):
    mu = jnp.mean(o, axis=-1, keepdims=True)
    var = jnp.mean(jnp.square(o - mu), axis=-1, keepdims=True)
    on = (o - mu) * lax.rsqrt(var + EPS)
    return (on * _silu(g.astype(F32))).astype(BF16)


def _ret_ctx_kernel(ld_ref, q_ref, k_ref, v_ref, g_ref, y_ref, st_ref):
    c = RET_CHUNK
    n = q_ref.shape[0]
    nc = n // c
    cs = _ret_consts(ld_ref)
    k_all = k_ref[...].astype(F32)
    v_all = v_ref[...]
    pos = lax.broadcasted_iota(jnp.int32, (n, 128), 0).astype(F32)
    w_f = jnp.exp((n - 1.0 - pos) * cs["lg_f"])[:, :1]
    w_b = jnp.exp(pos * cs["lg_b"])[:, :1]
    st_ref[0] = _dot_tn((k_all * w_f).astype(BF16), v_all)
    st_ref[1] = _dot_tn((k_all * w_b).astype(BF16), v_all)

    kv_f, kv_b = [], []
    for i in range(nc):
        kc = k_all[i * c:(i + 1) * c]
        vc = v_all[i * c:(i + 1) * c]
        kv_f.append(_dot_tn((kc * cs["kd_f"]).astype(BF16), vc))
        kv_b.append(_dot_tn((kc * cs["kd_b"]).astype(BF16), vc))
    for i in range(nc):
        s_f = None
        for jj in range(i):
            s_f = kv_f[jj] if s_f is None else s_f * cs["cd_f"] + kv_f[jj]
        s_b = None
        for jj in range(nc - 1, i, -1):
            s_b = kv_b[jj] if s_b is None else s_b * cs["cd_b"] + kv_b[jj]
        inter = []
        if s_f is not None:
            inter.append((s_f.astype(BF16), cs["qd_f"]))
        if s_b is not None:
            inter.append((s_b.astype(BF16), cs["qd_b"]))
        sl = slice(i * c, (i + 1) * c)
        o = _ret_chunk_out(q_ref[sl, :].astype(F32), k_all[sl], v_all[sl], cs["intra"], inter)
        y_ref[sl, :] = _ret_gate(o, g_ref[sl, :])


def _rope_chunk(x, cos_r, sin_r, cos_c, sin_c):
    half = x.shape[1] // 2
    xr, xc = x[:, :half], x[:, half:]
    rr = xr * cos_r + pltpu.roll(xr, half // 2, axis=1) * sin_r
    rc = xc * cos_c + pltpu.roll(xc, half // 2, axis=1) * sin_c
    return jnp.concatenate([rr, rc], axis=1)


def _ret_lat_kernel(ld_ref, st_ref, rr_ref, rc_ref, q_ref, k_ref, v_ref, g_ref, y_ref, sb_ref, s_ref):
    c = RET_CHUNK
    nc = q_ref.shape[0] // c
    cs = _ret_consts(ld_ref)
    cos_c, sin_c = rc_ref[0], rc_ref[1]
    upper = lax.broadcasted_iota(jnp.int32, (c, 128), 0) < GRID_W

    def rope_tables(i):
        r0 = pl.multiple_of(i * (c // GRID_W), c // GRID_W)
        cos_r = jnp.where(upper, rr_ref[0, pl.ds(r0, 1), :], rr_ref[0, pl.ds(r0 + 1, 1), :])
        sin_r = jnp.where(upper, rr_ref[1, pl.ds(r0, 1), :], rr_ref[1, pl.ds(r0 + 1, 1), :])
        return cos_r, sin_r

    def chunk(ref, i):
        return ref[pl.ds(pl.multiple_of(i * c, c), c), :]

    s_ref[...] = st_ref[1]

    def back(t, carry):
        i = nc - 1 - t
        sb_ref[i] = s_ref[...].astype(BF16)
        cos_r, sin_r = rope_tables(i)
        k = _rope_chunk(chunk(k_ref, i).astype(F32), cos_r, sin_r, cos_c, sin_c)
        s_ref[...] = s_ref[...] * cs["cd_b"] + _dot_tn((k * cs["kd_b"]).astype(BF16), chunk(v_ref, i))
        return carry

    lax.fori_loop(0, nc, back, 0)
    s_ref[...] = st_ref[0]

    def fwd(i, carry):
        cos_r, sin_r = rope_tables(i)
        q = _rope_chunk(chunk(q_ref, i).astype(F32), cos_r, sin_r, cos_c, sin_c)
        k = _rope_chunk(chunk(k_ref, i).astype(F32), cos_r, sin_r, cos_c, sin_c)
        v = chunk(v_ref, i)
        inter = [(s_ref[...].astype(BF16), cs["qd_f"]), (sb_ref[i], cs["qd_b"])]
        o = _ret_chunk_out(q, k, v, cs["intra"], inter)
        y_ref[pl.ds(pl.multiple_of(i * c, c), c), :] = _ret_gate(o, chunk(g_ref, i))
        s_ref[...] = s_ref[...] * cs["cd_f"] + _dot_tn((k * cs["kd_f"]).astype(BF16), v)
        return carry

    lax.fori_loop(0, nc, fwd, 0)


def _rope_tables(n_tok, dim):
    half = dim // 2
    freqs = ROPE_BASE ** (-jnp.arange(0, half, 2, dtype=F32) / half)

    def tables(pos):
        ang = pos[:, None] * freqs
        cos, sin = jnp.cos(ang), jnp.sin(ang)
        return jnp.stack([jnp.concatenate([cos, cos], axis=-1), jnp.concatenate([-sin, sin], axis=-1)])

    rows = jnp.arange(n_tok // GRID_W).astype(F32)
    cols = (jnp.arange(RET_CHUNK) % GRID_W).astype(F32)
    return tables(rows), tables(cols)


def _retention(qkvg_lat, qkvg_ctx, ld, rope_r, rope_c, *, batch):
    t_lat, width = qkvg_lat.shape
    t_ctx = qkvg_ctx.shape[0]
    n, lc = t_lat // batch, t_ctx // batch
    h = RET_HEADS
    vw = width // 3
    dk, dv = (width - 2 * vw) // (2 * h), vw // h
    kb, vb, gb = h, (2 * h * dk) // dv, (2 * h * dk) // dv + h
    nc = n // RET_CHUNK
    ld_b = jnp.broadcast_to(ld.astype(F32)[:, :, None, None], (2, h, 1, 128))
    ld_spec = pl.BlockSpec((2, None, 1, 128), lambda b, hh: (0, hh, 0, 0))

    y_ctx, states = pl.pallas_call(
        _ret_ctx_kernel,
        grid=(batch, h),
        in_specs=[
            ld_spec,
            pl.BlockSpec((lc, dk), lambda b, hh: (b, hh)),
            pl.BlockSpec((lc, dk), lambda b, hh: (b, kb + hh)),
            pl.BlockSpec((lc, dv), lambda b, hh: (b, vb + hh)),
            pl.BlockSpec((lc, dv), lambda b, hh: (b, gb + hh)),
        ],
        out_specs=[
            pl.BlockSpec((lc, dv), lambda b, hh: (b, hh)),
            pl.BlockSpec((None, None, 2, dk, dv), lambda b, hh: (b, hh, 0, 0, 0)),
        ],
        out_shape=[
            jax.ShapeDtypeStruct((t_ctx, h * dv), BF16),
            jax.ShapeDtypeStruct((batch, h, 2, dk, dv), F32),
        ],
        compiler_params=_cparams("parallel", "parallel"),
    )(ld_b, qkvg_ctx, qkvg_ctx, qkvg_ctx, qkvg_ctx)

    y_lat = pl.pallas_call(
        _ret_lat_kernel,
        grid=(batch, h),
        in_specs=[
            ld_spec,
            pl.BlockSpec((None, None, 2, dk, dv), lambda b, hh: (b, hh, 0, 0, 0)),
            pl.BlockSpec(rope_r.shape, lambda b, hh: (0, 0, 0)),
            pl.BlockSpec(rope_c.shape, lambda b, hh: (0, 0, 0)),
            pl.BlockSpec((n, dk), lambda b, hh: (b, hh)),
            pl.BlockSpec((n, dk), lambda b, hh: (b, kb + hh)),
            pl.BlockSpec((n, dv), lambda b, hh: (b, vb + hh)),
            pl.BlockSpec((n, dv), lambda b, hh: (b, gb + hh)),
        ],
        out_specs=pl.BlockSpec((n, dv), lambda b, hh: (b, hh)),
        out_shape=jax.ShapeDtypeStruct((t_lat, h * dv), BF16),
        scratch_shapes=[pltpu.VMEM((nc, dk, dv), BF16), pltpu.VMEM((dk, dv), F32)],
        compiler_params=_cparams("parallel", "parallel"),
    )(ld_b, states, rope_r, rope_c, qkvg_lat, qkvg_lat, qkvg_lat, qkvg_lat)
    return y_lat, y_ctx


def _na_geometry(rows):
    span = NA_QROWS + NA_KR
    nblk = rows // NA_QROWS
    return span, nblk


def _na_bias(rpb, rows):
    span, nblk = _na_geometry(rows)
    w = GRID_W
    variants = [(0, 0), (NA_QROWS, 0), (rows - NA_QROWS, rows - span)]
    dr_idx = np.zeros((3, NA_QROWS, span), np.int32)
    r_ok = np.zeros((3, NA_QROWS, span), bool)
    for vi, (r0, ks) in enumerate(variants):
        for ri in range(NA_QROWS):
            r = r0 + ri
            rs = min(max(r - NA_KR // 2, 0), rows - NA_KR)
            for kr in range(span):
                krow = ks + kr
                r_ok[vi, ri, kr] = rs <= krow < rs + NA_KR
                dr_idx[vi, ri, kr] = min(max(krow - r + NA_KR - 1, 0), 2 * NA_KR - 2)
    col = np.arange(w)
    cstart = np.clip(col - NA_KC // 2, 0, w - NA_KC)
    kc = np.arange(w)
    c_ok = (kc[None, :] >= cstart[:, None]) & (kc[None, :] < cstart[:, None] + NA_KC)
    dc_idx = np.clip(kc[None, :] - col[:, None] + NA_KC - 1, 0, 2 * NA_KC - 2)
    b = rpb.astype(F32)[:, dr_idx[:, :, :, None, None], dc_idx[None, None, None, :, :]]
    ok = r_ok[:, :, :, None, None] & c_ok[None, None, None, :, :]
    b = jnp.where(ok[None], b, NEG_BIG)
    b = b.transpose(0, 1, 2, 4, 3, 5)
    return b.reshape(rpb.shape[0], 3, NA_QROWS * w, span * w)


def _softmax_pv(parts):
    m = None
    for s, _ in parts:
        mi = jnp.max(s, axis=-1, keepdims=True)
        m = mi if m is None else jnp.maximum(m, mi)
    num, den = None, None
    for s, v in parts:
        p = jnp.exp(s - m)
        di = jnp.sum(p, axis=-1, keepdims=True)
        ni = jnp.dot(p.astype(BF16), v, preferred_element_type=F32)
        num = ni if num is None else num + ni
        den = di if den is None else den + di
    return num / den


def _na_kernel(bias_ref, q_ref, k_ref, v_ref, qc_ref, kc_ref, vc_ref, o_ref, oc_ref, *, rows, need_ctx):
    span, nblk = _na_geometry(rows)
    qn, sn = NA_QROWS * GRID_W, span * GRID_W
    kc, vc = kc_ref[...], vc_ref[...]

    if need_ctx:
        oc = _softmax_pv([(_dot_nt(qc_ref[...], kc), vc)])
        oc_ref[...] = oc.astype(oc_ref.dtype)
    else:
        oc_ref[...] = jnp.zeros_like(oc_ref)

    def block(i, carry):
        r0 = i * NA_QROWS
        ks = jnp.minimum(jnp.maximum(r0 - NA_KR // 2, 0), rows - span)
        variant = jnp.where(i == 0, 0, jnp.where(i == nblk - 1, 2, 1))
        q = q_ref[pl.ds(pl.multiple_of(i * qn, qn), qn), :]
        k0 = pl.multiple_of(ks * GRID_W, NA_QROWS * GRID_W)
        k = k_ref[pl.ds(k0, sn), :]
        v = v_ref[pl.ds(k0, sn), :]
        s_loc = _dot_nt(q, k) + bias_ref[variant]
        s_ctx = _dot_nt(q, kc)
        o = _softmax_pv([(s_loc, v), (s_ctx, vc)])
        o_ref[pl.ds(pl.multiple_of(i * qn, qn), qn), :] = o.astype(o_ref.dtype)
        return carry

    lax.fori_loop(0, nblk, block, 0)


def _na_attention(qkv_lat, qkv_ctx, bias, *, batch, need_ctx):
    t_lat, width = qkv_lat.shape
    t_ctx = qkv_ctx.shape[0]
    n, lc = t_lat // batch, t_ctx // batch
    h = NA_HEADS
    dh = width // (3 * h)
    rows = n // GRID_W
    return pl.pallas_call(
        functools.partial(_na_kernel, rows=rows, need_ctx=need_ctx),
        grid=(batch, h),
        in_specs=[
            pl.BlockSpec((None,) + bias.shape[1:], lambda b, hh: (hh, 0, 0, 0)),
            pl.BlockSpec((n, dh), lambda b, hh: (b, hh)),
            pl.BlockSpec((n, dh), lambda b, hh: (b, h + hh)),
            pl.BlockSpec((n, dh), lambda b, hh: (b, 2 * h + hh)),
            pl.BlockSpec((lc, dh), lambda b, hh: (b, hh)),
            pl.BlockSpec((lc, dh), lambda b, hh: (b, h + hh)),
            pl.BlockSpec((lc, dh), lambda b, hh: (b, 2 * h + hh)),
        ],
        out_specs=[
            pl.BlockSpec((n, dh), lambda b, hh: (b, hh)),
            pl.BlockSpec((lc, dh), lambda b, hh: (b, hh)),
        ],
        out_shape=[
            jax.ShapeDtypeStruct((t_lat, h * dh), BF16),
            jax.ShapeDtypeStruct((t_ctx, h * dh), BF16),
        ],
        compiler_params=_cparams("parallel", "parallel"),
    )(bias, qkv_lat, qkv_lat, qkv_lat, qkv_ctx, qkv_ctx, qkv_ctx)


def kernel(x, c, ctx, c_ctx, ada_w, ada_b, norm_g, ffn_w_gu, ffn_w_down, ret_w_in, ret_w_out,
           ret_log_decay, na_w_in, na_w_out, na_rpb, final_g):
    batch, n_lat, d = x.shape
    l_ctx = ctx.shape[1]
    depth = ada_w.shape[0]
    n_mixers = 2
    assert n_lat % GRID_W == 0 and n_lat % PROJ_TM == 0 and (batch * l_ctx) % FFN_TM == 0

    xt = x.reshape(batch * n_lat, d)
    ct = ctx.reshape(batch * l_ctx, d)

    cc = jnp.zeros((8, d), F32).at[:batch].set(c).at[batch].set(c_ctx)
    mod = _modulation(cc, ada_w, ada_b).reshape(depth, 8, N_MOD, d)
    lat = dict(rows_per_mod=n_lat, mod_base=0)
    cx = dict(rows_per_mod=batch * l_ctx, mod_base=batch)

    ret_dk = d // RET_HEADS
    rope_r, rope_c = _rope_tables(n_lat, ret_dk)
    ret_w = ret_w_in.shape[2]
    ret_scale = jnp.ones((1, ret_w), F32).at[:, d:2 * d].set(ret_dk ** -0.5)
    na_dh = d // NA_HEADS
    na_scale = jnp.ones((1, 3 * d), F32).at[:, :d].set(na_dh ** -0.5)

    for i in range(depth):
        last = i == depth - 1
        j = i // n_mixers
        wgu1, wd1 = _prep_ffn_weights(ffn_w_gu[i, 0], ffn_w_down[i, 0])
        wgu2, wd2 = _prep_ffn_weights(ffn_w_gu[i, 1], ffn_w_down[i, 1])

        xt = _ffn(xt, mod, norm_g, wgu1, wd1, layer=i, which=0, **lat)
        ct = _ffn(ct, mod, norm_g, wgu1, wd1, layer=i, which=0, **cx)

        if i % n_mixers == 0:
            w_in, w_out = ret_w_in[j].astype(BF16), ret_w_out[j].astype(BF16)
            p_lat = _inproj(xt, mod, norm_g, w_in, ret_scale, layer=i, **lat)
            p_ctx = _inproj(ct, mod, norm_g, w_in, ret_scale, layer=i, **cx)
            y_lat, y_ctx = _retention(p_lat, p_ctx, ret_log_decay[j], rope_r, rope_c, batch=batch)
        else:
            w_in, w_out = na_w_in[j].astype(BF16), na_w_out[j].astype(BF16)
            p_lat = _inproj(xt, mod, norm_g, w_in, na_scale, layer=i, **lat)
            p_ctx = _inproj(ct, mod, norm_g, w_in, na_scale, layer=i, **cx)
            bias = _na_bias(na_rpb[j], n_lat // GRID_W)
            y_lat, y_ctx = _na_attention(p_lat, p_ctx, bias, batch=batch, need_ctx=not last)

        xt = _outproj(y_lat, w_out, xt, mod, layer=i, **lat)
        xt = _ffn(xt, mod, norm_g, wgu2, wd2, layer=i, which=1, final_g=final_g if last else None, **lat)
        if not last:
            ct = _outproj(y_ctx, w_out, ct, mod, layer=i, **cx)
            ct = _ffn(ct, mod, norm_g, wgu2, wd2, layer=i, which=1, **cx)

    return xt.reshape(batch, n_lat, d)
```

```python
import functools

import numpy as np
import jax
import jax.numpy as jnp
from jax import lax
from jax.experimental import pallas as pl
from jax.experimental.pallas import tpu as pltpu

F32 = jnp.float32
BF16 = jnp.bfloat16

GRID_W = 64
N_MOD = 9
RET_HEADS = 8
NA_HEADS = 16
NA_KR = 8
NA_KC = 16
ROPE_BASE = 10000.0
EPS = 1e-6

V7X_VMEM_BYTES = 64 * 1024 * 1024
VMEM_LIMIT_BYTES = V7X_VMEM_BYTES - 8 * 1024 * 1024
NEG_BIG = -1e30

RET_CHUNK = 256
FFN_TM = 512
FFN_FC = 512
PROJ_TM = 2048
PROJ_TN = 1024
OUT_TM = 512
MOD_TN = 1024
NORM_ROWS = 256
PROJ_ROWS = 512
NA_QROWS = 4


def _cparams(*sem):
    return pltpu.CompilerParams(dimension_semantics=sem, vmem_limit_bytes=VMEM_LIMIT_BYTES)


def _silu(x):
    return x * (1.0 / (1.0 + jnp.exp(-x)))


def _ada_norm_tile(x, g, shift, scale):
    y = x * lax.rsqrt(jnp.mean(x * x, axis=-1, keepdims=True) + EPS) * g
    return y * (1.0 + scale) + shift


def _ada_norm_to(x_ref, h_ref, g, shift, scale):
    def body(r, carry):
        sl = pl.ds(pl.multiple_of(r * NORM_ROWS, NORM_ROWS), NORM_ROWS)
        h_ref[sl, :] = _ada_norm_tile(x_ref[sl, :], g, shift, scale).astype(BF16)
        return carry

    lax.fori_loop(0, x_ref.shape[0] // NORM_ROWS, body, 0)


def _mod_kernel(c_ref, w_ref, b_ref, o_ref):
    s = _silu(c_ref[...]).astype(BF16)
    o_ref[...] = jnp.dot(s, w_ref[...].astype(BF16), preferred_element_type=F32) + b_ref[...]


def _modulation(cc, ada_w, ada_b):
    depth, d, n = ada_w.shape
    return pl.pallas_call(
        _mod_kernel,
        grid=(depth, n // MOD_TN),
        in_specs=[
            pl.BlockSpec((8, d), lambda l, j: (0, 0)),
            pl.BlockSpec((None, d, MOD_TN), lambda l, j: (l, 0, j)),
            pl.BlockSpec((None, 1, MOD_TN), lambda l, j: (l, 0, j)),
        ],
        out_specs=pl.BlockSpec((None, 8, MOD_TN), lambda l, j: (l, 0, j)),
        out_shape=jax.ShapeDtypeStruct((depth, 8, n), F32),
        compiler_params=_cparams("arbitrary", "arbitrary"),
    )(cc, ada_w, ada_b.reshape(depth, 1, n))


def _mod_spec(layer, rows_per_mod, mod_base, tm, d):
    tiles_per_mod = rows_per_mod // tm
    return pl.BlockSpec((None, None, N_MOD, d), lambda t, *_: (layer, mod_base + t // tiles_per_mod, 0, 0))


def _ffn_kernel(*refs, mod_off, g_row, final):
    if final:
        (x_ref, mod_ref, g_ref, wg_ref, wu_ref, wd_ref, wgt_ref, wut_ref, wdt_ref, fg_ref, o_ref, h_ref) = refs
    else:
        (x_ref, mod_ref, g_ref, wg_ref, wu_ref, wd_ref, wgt_ref, wut_ref, wdt_ref, o_ref, h_ref) = refs
    j = pl.program_id(1)
    n_main = pl.num_programs(1) - 1

    def chunk(wg, wu, wd):
        h = h_ref[...]
        gate = jnp.dot(h, wg, preferred_element_type=F32)
        up = jnp.dot(h, wu, preferred_element_type=F32)
        return jnp.dot((_silu(gate) * up).astype(BF16), wd, preferred_element_type=F32)

    @pl.when(j == 0)
    def _():
        _ada_norm_to(x_ref, h_ref, g_ref[g_row:g_row + 1, :],
                     mod_ref[mod_off:mod_off + 1, :], mod_ref[mod_off + 1:mod_off + 2, :])
        o_ref[...] = jnp.zeros_like(o_ref)

    @pl.when(j < n_main)
    def _():
        o_ref[...] += chunk(wg_ref[...], wu_ref[...], wd_ref[...])

    @pl.when(j == n_main)
    def _():
        acc = o_ref[...] + chunk(wgt_ref[...], wut_ref[...], wdt_ref[...])
        o = x_ref[...] + (0.5 * mod_ref[mod_off + 2:mod_off + 3, :]) * acc
        if final:
            o = o * lax.rsqrt(jnp.mean(o * o, axis=-1, keepdims=True) + EPS) * fg_ref[...]
        o_ref[...] = o


def _ffn(x, mod, norm_g, w_gu, w_down, *, layer, which, rows_per_mod, mod_base, final_g=None):
    t, d = x.shape
    f = w_down.shape[0]
    fc = FFN_FC
    n_main = (f - 1) // fc
    ft = f - n_main * fc
    assert ft % 128 == 0 and n_main >= 1
    tm = min(FFN_TM, t)
    final = final_g is not None
    el = pl.Element

    def main(j):
        return jnp.minimum(j, n_main - 1) * fc

    in_specs = [
        pl.BlockSpec((tm, d), lambda i, j: (i, 0)),
        _mod_spec(layer, rows_per_mod, mod_base, tm, d),
        pl.BlockSpec((None, 3, d), lambda i, j: (layer, 0, 0)),
        pl.BlockSpec((el(d), el(fc)), lambda i, j: (0, main(j))),
        pl.BlockSpec((el(d), el(fc)), lambda i, j: (0, pl.multiple_of(f + main(j), 128))),
        pl.BlockSpec((el(fc), el(d)), lambda i, j: (main(j), 0)),
        pl.BlockSpec((el(d), el(ft)), lambda i, j: (0, n_main * fc)),
        pl.BlockSpec((el(d), el(ft)), lambda i, j: (0, f + n_main * fc)),
        pl.BlockSpec((el(ft), el(d)), lambda i, j: (n_main * fc, 0)),
    ]
    args = [x, mod, norm_g, w_gu, w_gu, w_down, w_gu, w_gu, w_down]
    if final:
        in_specs.append(pl.BlockSpec((1, d), lambda i, j: (0, 0)))
        args.append(final_g.reshape(1, d))
    return pl.pallas_call(
        functools.partial(_ffn_kernel, mod_off=6 * which, g_row=2 * which, final=final),
        grid=(t // tm, n_main + 1),
        in_specs=in_specs,
        out_specs=pl.BlockSpec((tm, d), lambda i, j: (i, 0)),
        out_shape=jax.ShapeDtypeStruct((t, d), F32),
        scratch_shapes=[pltpu.VMEM((tm, d), BF16)],
        compiler_params=_cparams("parallel", "arbitrary"),
    )(*args)


def _inproj_kernel(x_ref, mod_ref, g_ref, w_ref, cs_ref, o_ref, h_ref):
    @pl.when(pl.program_id(1) == 0)
    def _():
        _ada_norm_to(x_ref, h_ref, g_ref[1:2, :], mod_ref[3:4, :], mod_ref[4:5, :])

    for r in range(0, o_ref.shape[0], PROJ_ROWS):
        acc = jnp.dot(h_ref[r:r + PROJ_ROWS, :], w_ref[...], preferred_element_type=F32)
        o_ref[r:r + PROJ_ROWS, :] = (acc * cs_ref[...]).astype(o_ref.dtype)


def _inproj(x, mod, norm_g, w, col_scale, *, layer, rows_per_mod, mod_base):
    t, d = x.shape
    n = w.shape[1]
    tm = min(PROJ_TM, t)
    return pl.pallas_call(
        _inproj_kernel,
        grid=(t // tm, n // PROJ_TN),
        in_specs=[
            pl.BlockSpec((tm, d), lambda i, j: (i, 0), pipeline_mode=pl.Buffered(1)),
            _mod_spec(layer, rows_per_mod, mod_base, tm, d),
            pl.BlockSpec((None, 3, d), lambda i, j: (layer, 0, 0)),
            pl.BlockSpec((d, PROJ_TN), lambda i, j: (0, j)),
            pl.BlockSpec((1, PROJ_TN), lambda i, j: (0, j)),
        ],
        out_specs=pl.BlockSpec((tm, PROJ_TN), lambda i, j: (i, j)),
        out_shape=jax.ShapeDtypeStruct((t, n), BF16),
        scratch_shapes=[pltpu.VMEM((tm, d), BF16)],
        compiler_params=_cparams("parallel", "arbitrary"),
    )(x, mod, norm_g, w, col_scale)


def _outproj_kernel(y_ref, w_ref, x_ref, mod_ref, o_ref):
    acc = jnp.dot(y_ref[...], w_ref[...], preferred_element_type=F32)
    o_ref[...] = x_ref[...] + mod_ref[5:6, :] * acc


def _outproj(y, w, x, mod, *, layer, rows_per_mod, mod_base):
    t, d = x.shape
    kd = y.shape[1]
    tm = min(OUT_TM, t)
    return pl.pallas_call(
        _outproj_kernel,
        grid=(t // tm,),
        in_specs=[
            pl.BlockSpec((tm, kd), lambda i: (i, 0)),
            pl.BlockSpec((kd, d), lambda i: (0, 0), pipeline_mode=pl.Buffered(1)),
            pl.BlockSpec((tm, d), lambda i: (i, 0)),
            _mod_spec(layer, rows_per_mod, mod_base, tm, d),
        ],
        out_specs=pl.BlockSpec((tm, d), lambda i: (i, 0)),
        out_shape=jax.ShapeDtypeStruct((t, d), F32),
        compiler_params=_cparams("parallel"),
    )(y, w, x, mod)


def _dot_nt(a, b):
    return lax.dot_general(a, b, (((1,), (1,)), ((), ())), preferred_element_type=F32)


def _dot_tn(a, b):
    return lax.dot_general(a, b, (((0,), (0,)), ((), ())), preferred_element_type=F32)


def _ret_consts(ld_ref):
    c = RET_CHUNK
    lg_f = -jnp.exp(ld_ref[0])[:, :1]
    lg_b = -jnp.exp(ld_ref[1])[:, :1]
    row = lax.broadcasted_iota(jnp.int32, (c, c), 0)
    col = lax.broadcasted_iota(jnp.int32, (c, c), 1)
    diff = (row - col).astype(F32)
    m_f = jnp.where(diff >= 0, jnp.exp(jnp.maximum(diff, 0.0) * lg_f), 0.0)
    m_b = jnp.where(diff <= 0, jnp.exp(jnp.maximum(-diff, 0.0) * lg_b), 0.0)
    pos = lax.broadcasted_iota(jnp.int32, (c, 1), 0).astype(F32)
    return dict(
        intra=m_f + m_b,
        qd_f=jnp.exp((pos + 1.0) * lg_f),
        qd_b=jnp.exp((c - pos) * lg_b),
        kd_f=jnp.exp((c - 1.0 - pos) * lg_f),
        kd_b=jnp.exp(pos * lg_b),
        cd_f=jnp.exp(float(c) * lg_f),
        cd_b=jnp.exp(float(c) * lg_b),
        lg_f=lg_f, lg_b=lg_b,
    )


def _ret_chunk_out(q, k, v, intra, inter):
    qb = q.astype(BF16)
    s = _dot_nt(qb, k.astype(BF16)) * intra
    o = jnp.dot(s.astype(BF16), v, preferred_element_type=F32)
    for state_bf16, qd in inter:
        o = o + jnp.dot(qb, state_bf16, preferred_element_type=F32) * qd
    return o


def _ret_gate(o, g):
    mu = jnp.mean(o, axis=-1, keepdims=True)
    var = jnp.mean(jnp.square(o - mu), axis=-1, keepdims=True)
    on = (o - mu) * lax.rsqrt(var + EPS)
    return (on * _silu(g.astype(F32))).astype(BF16)


def _ret_ctx_kernel(ld_ref, q_ref, k_ref, v_ref, g_ref, y_ref, st_ref):
    c = RET_CHUNK
    n = q_ref.shape[0]
    nc = n // c
    cs = _ret_consts(ld_ref)
    k_all = k_ref[...].astype(F32)
    v_all = v_ref[...]
    pos = lax.broadcasted_iota(jnp.int32, (n, 1), 0).astype(F32)
    w_f = jnp.exp((n - 1.0 - pos) * cs["lg_f"])
    w_b = jnp.exp(pos * cs["lg_b"])
    st_ref[0] = _dot_tn((k_all * w_f).astype(BF16), v_all)
    st_ref[1] = _dot_tn((k_all * w_b).astype(BF16), v_all)

    kv_f, kv_b = [], []
    for i in range(nc):
        kc = k_all[i * c:(i + 1) * c]
        vc = v_all[i * c:(i + 1) * c]
        kv_f.append(_dot_tn((kc * cs["kd_f"]).astype(BF16), vc) if i < nc - 1 else None)
        kv_b.append(_dot_tn((kc * cs["kd_b"]).astype(BF16), vc) if i > 0 else None)
    for i in range(nc):
        s_f = None
        for jj in range(i):
            s_f = kv_f[jj] if s_f is None else s_f * cs["cd_f"] + kv_f[jj]
        s_b = None
        for jj in range(nc - 1, i, -1):
            s_b = kv_b[jj] if s_b is None else s_b * cs["cd_b"] + kv_b[jj]
        inter = []
        if s_f is not None:
            inter.append((s_f.astype(BF16), cs["qd_f"]))
        if s_b is not None:
            inter.append((s_b.astype(BF16), cs["qd_b"]))
        sl = slice(i * c, (i + 1) * c)
        o = _ret_chunk_out(q_ref[sl, :].astype(F32), k_all[sl], v_all[sl], cs["intra"], inter)
        y_ref[sl, :] = _ret_gate(o, g_ref[sl, :])


def _rope_chunk(x, cos_r, sin_r, cos_c, sin_c):
    half = x.shape[1] // 2
    xr, xc = x[:, :half], x[:, half:]
    rr = xr * cos_r + pltpu.roll(xr, half // 2, axis=1) * sin_r
    rc = xc * cos_c + pltpu.roll(xc, half // 2, axis=1) * sin_c
    return jnp.concatenate([rr, rc], axis=1)


def _ret_lat_kernel(ld_ref, st_ref, rr_ref, rc_ref, q_ref, k_ref, v_ref, g_ref, y_ref, sb_ref, s_ref):
    c = RET_CHUNK
    rows_per_chunk = c // GRID_W
    nc = q_ref.shape[0] // c
    half = rr_ref.shape[2]
    cs = _ret_consts(ld_ref)
    cos_c, sin_c = rc_ref[0], rc_ref[1]

    def rope_tables(i):
        r0 = i * rows_per_chunk

        def rows(t):
            return jnp.concatenate(
                [jnp.broadcast_to(rr_ref[t, pl.ds(r0 + r, 1), :], (GRID_W, half)) for r in range(rows_per_chunk)], axis=0)

        return rows(0), rows(1)

    def chunk(ref, i):
        return ref[pl.ds(pl.multiple_of(i * c, c), c), :]

    s_ref[...] = st_ref[1]

    def back(t, carry):
        i = nc - 1 - t
        sb_ref[i] = s_ref[...].astype(BF16)
        cos_r, sin_r = rope_tables(i)
        k = _rope_chunk(chunk(k_ref, i).astype(F32), cos_r, sin_r, cos_c, sin_c)
        s_ref[...] = s_ref[...] * cs["cd_b"] + _dot_tn((k * cs["kd_b"]).astype(BF16), chunk(v_ref, i))
        return carry

    lax.fori_loop(0, nc, back, 0)
    s_ref[...] = st_ref[0]

    def fwd(i, carry):
        cos_r, sin_r = rope_tables(i)
        q = _rope_chunk(chunk(q_ref, i).astype(F32), cos_r, sin_r, cos_c, sin_c)
        k = _rope_chunk(chunk(k_ref, i).astype(F32), cos_r, sin_r, cos_c, sin_c)
        v = chunk(v_ref, i)
        inter = [(s_ref[...].astype(BF16), cs["qd_f"]), (sb_ref[i], cs["qd_b"])]
        o = _ret_chunk_out(q, k, v, cs["intra"], inter)
        y_ref[pl.ds(pl.multiple_of(i * c, c), c), :] = _ret_gate(o, chunk(g_ref, i))
        s_ref[...] = s_ref[...] * cs["cd_f"] + _dot_tn((k * cs["kd_f"]).astype(BF16), v)
        return carry

    lax.fori_loop(0, nc, fwd, 0)


def _rope_tables(n_tok, dim):
    half = dim // 2
    freqs = ROPE_BASE ** (-jnp.arange(0, half, 2, dtype=F32) / half)

    def tables(pos):
        ang = pos[:, None] * freqs
        cos, sin = jnp.cos(ang), jnp.sin(ang)
        return jnp.stack([jnp.concatenate([cos, cos], axis=-1), jnp.concatenate([-sin, sin], axis=-1)])

    rows = jnp.arange(n_tok // GRID_W).astype(F32)
    cols = (jnp.arange(RET_CHUNK) % GRID_W).astype(F32)
    return tables(rows), tables(cols)


def _retention(qkvg_lat, qkvg_ctx, ld, rope_r, rope_c, *, batch):
    t_lat, width = qkvg_lat.shape
    t_ctx = qkvg_ctx.shape[0]
    n, lc = t_lat // batch, t_ctx // batch
    h = RET_HEADS
    vw = width // 3
    dk, dv = (width - 2 * vw) // (2 * h), vw // h
    kb, vb, gb = h, (2 * h * dk) // dv, (2 * h * dk) // dv + h
    assert n % RET_CHUNK == 0 and lc % RET_CHUNK == 0 and RET_CHUNK % GRID_W == 0
    nc = n // RET_CHUNK
    ld_b = jnp.broadcast_to(ld.astype(F32)[:, :, None, None], (2, h, 1, 128))
    ld_spec = pl.BlockSpec((2, None, 1, 128), lambda b, hh: (0, hh, 0, 0))

    y_ctx, states = pl.pallas_call(
        _ret_ctx_kernel,
        grid=(batch, h),
        in_specs=[
            ld_spec,
            pl.BlockSpec((lc, dk), lambda b, hh: (b, hh)),
            pl.BlockSpec((lc, dk), lambda b, hh: (b, kb + hh)),
            pl.BlockSpec((lc, dv), lambda b, hh: (b, vb + hh)),
            pl.BlockSpec((lc, dv), lambda b, hh: (b, gb + hh)),
        ],
        out_specs=[
            pl.BlockSpec((lc, dv), lambda b, hh: (b, hh)),
            pl.BlockSpec((None, None, 2, dk, dv), lambda b, hh: (b, hh, 0, 0, 0)),
        ],
        out_shape=[
            jax.ShapeDtypeStruct((t_ctx, h * dv), BF16),
            jax.ShapeDtypeStruct((batch, h, 2, dk, dv), F32),
        ],
        compiler_params=_cparams("parallel", "parallel"),
    )(ld_b, qkvg_ctx, qkvg_ctx, qkvg_ctx, qkvg_ctx)

    y_lat = pl.pallas_call(
        _ret_lat_kernel,
        grid=(batch, h),
        in_specs=[
            ld_spec,
            pl.BlockSpec((None, None, 2, dk, dv), lambda b, hh: (b, hh, 0, 0, 0)),
            pl.BlockSpec(rope_r.shape, lambda b, hh: (0, 0, 0)),
            pl.BlockSpec(rope_c.shape, lambda b, hh: (0, 0, 0)),
            pl.BlockSpec((n, dk), lambda b, hh: (b, hh)),
            pl.BlockSpec((n, dk), lambda b, hh: (b, kb + hh)),
            pl.BlockSpec((n, dv), lambda b, hh: (b, vb + hh)),
            pl.BlockSpec((n, dv), lambda b, hh: (b, gb + hh)),
        ],
        out_specs=pl.BlockSpec((n, dv), lambda b, hh: (b, hh)),
        out_shape=jax.ShapeDtypeStruct((t_lat, h * dv), BF16),
        scratch_shapes=[pltpu.VMEM((nc, dk, dv), BF16), pltpu.VMEM((dk, dv), F32)],
        compiler_params=_cparams("parallel", "parallel"),
    )(ld_b, states, rope_r, rope_c, qkvg_lat, qkvg_lat, qkvg_lat, qkvg_lat)
    return y_lat, y_ctx


NA_SPAN = NA_QROWS + NA_KR


def _na_bias(rpb, rows):
    w, span = GRID_W, NA_SPAN
    col = np.arange(w)
    cstart = np.clip(col - NA_KC // 2, 0, w - NA_KC)
    c_ok = (col[None, :] >= cstart[:, None]) & (col[None, :] < cstart[:, None] + NA_KC)
    p = jnp.pad(rpb.astype(F32), ((0, 0), (0, 0), (w - 1, w - 1)), constant_values=NEG_BIG)
    off = NA_KC - 1 + w - 1
    tiles = jnp.stack([p[:, :, off - c:off - c + w] for c in range(w)], axis=2)
    tiles = jnp.where(c_ok[None, None], tiles, NEG_BIG)
    masked = jnp.full((rpb.shape[0], w, w), NEG_BIG, F32)
    variants = [(0, 0), (NA_QROWS, 0), (rows - NA_QROWS, rows - span)]
    out = []
    for r0, ks in variants:
        q_rows = []
        for ri in range(NA_QROWS):
            r = r0 + ri
            rs = min(max(r - NA_KR // 2, 0), rows - NA_KR)
            row_tiles = []
            for kr in range(span):
                krow = ks + kr
                row_tiles.append(tiles[:, krow - r + NA_KR - 1] if rs <= krow < rs + NA_KR else masked)
            q_rows.append(jnp.concatenate(row_tiles, axis=-1))
        out.append(jnp.concatenate(q_rows, axis=1))
    return jnp.stack(out, axis=1)


def _softmax_pv(parts):
    m = None
    for s, _ in parts:
        mi = jnp.max(s, axis=-1, keepdims=True)
        m = mi if m is None else jnp.maximum(m, mi)
    num, den = None, None
    for s, v in parts:
        p = jnp.exp(s - m)
        di = jnp.sum(p, axis=-1, keepdims=True)
        ni = jnp.dot(p.astype(BF16), v, preferred_element_type=F32)
        num = ni if num is None else num + ni
        den = di if den is None else den + di
    return num / den


def _na_kernel(bias_ref, q_ref, k_ref, v_ref, qc_ref, kc_ref, vc_ref, o_ref, oc_ref, *, rows, need_ctx):
    nblk = rows // NA_QROWS
    qn, sn = NA_QROWS * GRID_W, NA_SPAN * GRID_W
    kc, vc = kc_ref[...], vc_ref[...]

    if need_ctx:
        oc = _softmax_pv([(_dot_nt(qc_ref[...], kc), vc)])
        oc_ref[...] = oc.astype(oc_ref.dtype)
    else:
        oc_ref[...] = jnp.zeros_like(oc_ref)

    def block(i, carry):
        r0 = i * NA_QROWS
        ks = jnp.minimum(jnp.maximum(r0 - NA_KR // 2, 0), rows - NA_SPAN)
        variant = jnp.where(i == 0, 0, jnp.where(i == nblk - 1, 2, 1))
        q = q_ref[pl.ds(pl.multiple_of(i * qn, qn), qn), :]
        k0 = pl.multiple_of(ks * GRID_W, NA_QROWS * GRID_W)
        k = k_ref[pl.ds(k0, sn), :]
        v = v_ref[pl.ds(k0, sn), :]
        s_loc = _dot_nt(q, k) + bias_ref[variant]
        s_ctx = _dot_nt(q, kc)
        o = _softmax_pv([(s_loc, v), (s_ctx, vc)])
        o_ref[pl.ds(pl.multiple_of(i * qn, qn), qn), :] = o.astype(o_ref.dtype)
        return carry

    lax.fori_loop(0, nblk, block, 0, unroll=2)


def _na_attention(qkv_lat, qkv_ctx, bias, *, batch, need_ctx):
    t_lat, width = qkv_lat.shape
    t_ctx = qkv_ctx.shape[0]
    n, lc = t_lat // batch, t_ctx // batch
    h = NA_HEADS
    dh = width // (3 * h)
    rows = n // GRID_W
    assert rows % NA_QROWS == 0 and (NA_KR // 2) % NA_QROWS == 0 and rows >= NA_SPAN
    return pl.pallas_call(
        functools.partial(_na_kernel, rows=rows, need_ctx=need_ctx),
        grid=(batch, h),
        in_specs=[
            pl.BlockSpec((None,) + bias.shape[1:], lambda b, hh: (hh, 0, 0, 0)),
            pl.BlockSpec((n, dh), lambda b, hh: (b, hh)),
            pl.BlockSpec((n, dh), lambda b, hh: (b, h + hh)),
            pl.BlockSpec((n, dh), lambda b, hh: (b, 2 * h + hh)),
            pl.BlockSpec((lc, dh), lambda b, hh: (b, hh)),
            pl.BlockSpec((lc, dh), lambda b, hh: (b, h + hh)),
            pl.BlockSpec((lc, dh), lambda b, hh: (b, 2 * h + hh)),
        ],
        out_specs=[
            pl.BlockSpec((n, dh), lambda b, hh: (b, hh)),
            pl.BlockSpec((lc, dh), lambda b, hh: (b, hh)),
        ],
        out_shape=[
            jax.ShapeDtypeStruct((t_lat, h * dh), BF16),
            jax.ShapeDtypeStruct((t_ctx, h * dh), BF16),
        ],
        compiler_params=_cparams("parallel", "parallel"),
    )(bias, qkv_lat, qkv_lat, qkv_lat, qkv_ctx, qkv_ctx, qkv_ctx)


def kernel(x, c, ctx, c_ctx, ada_w, ada_b, norm_g, ffn_w_gu, ffn_w_down, ret_w_in, ret_w_out,
           ret_log_decay, na_w_in, na_w_out, na_rpb, final_g):
    batch, n_lat, d = x.shape
    l_ctx = ctx.shape[1]
    depth = ada_w.shape[0]
    n_mixers = 2
    assert n_lat % GRID_W == 0 and n_lat % PROJ_TM == 0 and (batch * l_ctx) % FFN_TM == 0

    xt = x.reshape(batch * n_lat, d)
    ct = ctx.reshape(batch * l_ctx, d)

    cc = jnp.zeros((8, d), F32).at[:batch].set(c).at[batch].set(c_ctx)
    mod = _modulation(cc, ada_w, ada_b).reshape(depth, 8, N_MOD, d)
    lat = dict(rows_per_mod=n_lat, mod_base=0)
    cx = dict(rows_per_mod=batch * l_ctx, mod_base=batch)

    ret_dk = d // RET_HEADS
    rope_r, rope_c = _rope_tables(n_lat, ret_dk)
    ret_w = ret_w_in.shape[2]
    ret_scale = jnp.ones((1, ret_w), F32).at[:, d:2 * d].set(ret_dk ** -0.5)
    na_dh = d // NA_HEADS
    na_scale = jnp.ones((1, 3 * d), F32).at[:, :d].set(na_dh ** -0.5)

    for i in range(depth):
        last = i == depth - 1
        j = i // n_mixers
        w_gu1, w_d1 = ffn_w_gu[i, 0].astype(BF16), ffn_w_down[i, 0].astype(BF16)
        w_gu2, w_d2 = ffn_w_gu[i, 1].astype(BF16), ffn_w_down[i, 1].astype(BF16)

        xt = _ffn(xt, mod, norm_g, w_gu1, w_d1, layer=i, which=0, **lat)
        ct = _ffn(ct, mod, norm_g, w_gu1, w_d1, layer=i, which=0, **cx)

        if i % n_mixers == 0:
            w_in, w_out = ret_w_in[j].astype(BF16), ret_w_out[j].astype(BF16)
            p_lat = _inproj(xt, mod, norm_g, w_in, ret_scale, layer=i, **lat)
            p_ctx = _inproj(ct, mod, norm_g, w_in, ret_scale, layer=i, **cx)
            y_lat, y_ctx = _retention(p_lat, p_ctx, ret_log_decay[j], rope_r, rope_c, batch=batch)
        else:
            w_in, w_out = na_w_in[j].astype(BF16), na_w_out[j].astype(BF16)
            p_lat = _inproj(xt, mod, norm_g, w_in, na_scale, layer=i, **lat)
            p_ctx = _inproj(ct, mod, norm_g, w_in, na_scale, layer=i, **cx)
            bias = _na_bias(na_rpb[j], n_lat // GRID_W)
            y_lat, y_ctx = _na_attention(p_lat, p_ctx, bias, batch=batch, need_ctx=not last)

        xt = _outproj(y_lat, w_out, xt, mod, layer=i, **lat)
        xt = _ffn(xt, mod, norm_g, w_gu2, w_d2, layer=i, which=1, final_g=final_g if last else None, **lat)
        if not last:
            ct = _outproj(y_ctx, w_out, ct, mod, layer=i, **cx)
            ct = _ffn(ct, mod, norm_g, w_gu2, w_d2, layer=i, which=1, **cx)

    return xt.reshape(batch, n_lat, d)
```

```python
import functools

import numpy as np
import jax
import jax.numpy as jnp
from jax import lax
from jax.experimental import pallas as pl
from jax.experimental.pallas import tpu as pltpu

F32 = jnp.float32
BF16 = jnp.bfloat16

GRID_W = 64
N_MOD = 9
RET_HEADS = 8
NA_HEADS = 16
NA_KR = 8
NA_KC = 16
ROPE_BASE = 10000.0
EPS = 1e-6

V7X_VMEM_BYTES = 64 * 1024 * 1024
VMEM_LIMIT_BYTES = V7X_VMEM_BYTES - 8 * 1024 * 1024
NEG_BIG = -1e30

RET_CHUNK = 256
FFN_TM = 512
FFN_ROWS = 512
FFN_FC = 512
FFN_WBUFS = 2
CONV_ROWS = 16
PROJ_TM = 2048
PROJ_TN = 1024
OUT_TM = 512
MOD_TN = 1024
NORM_ROWS = 16
NORM_UNROLL = 8
EPI_COLS = 512
PROJ_ROWS = 512
NA_QROWS = 4


def _cparams(*sem):
    return pltpu.CompilerParams(dimension_semantics=sem, vmem_limit_bytes=VMEM_LIMIT_BYTES)


def _silu(x):
    return x * (1.0 / (1.0 + jnp.exp(-x)))


def _ada_norm_tile(x, g, shift, scale):
    y = x * lax.rsqrt(jnp.mean(x * x, axis=-1, keepdims=True) + EPS) * g
    return y * (1.0 + scale) + shift


def _ada_norm_to(x_ref, h_ref, g, shift, scale):
    def body(r, carry):
        sl = pl.ds(pl.multiple_of(r * NORM_ROWS, NORM_ROWS), NORM_ROWS)
        h_ref[sl, :] = _ada_norm_tile(x_ref[sl, :], g, shift, scale).astype(BF16)
        return carry

    lax.fori_loop(0, x_ref.shape[0] // NORM_ROWS, body, 0, unroll=NORM_UNROLL)


def _mod_kernel(c_ref, w_ref, b_ref, o_ref):
    s = _silu(c_ref[...]).astype(BF16)
    o_ref[...] = jnp.dot(s, w_ref[...].astype(BF16), preferred_element_type=F32) + b_ref[...]


def _modulation(cc, ada_w, ada_b):
    depth, d, n = ada_w.shape
    return pl.pallas_call(
        _mod_kernel,
        grid=(depth, n // MOD_TN),
        in_specs=[
            pl.BlockSpec((8, d), lambda l, j: (0, 0)),
            pl.BlockSpec((None, d, MOD_TN), lambda l, j: (l, 0, j)),
            pl.BlockSpec((None, 1, MOD_TN), lambda l, j: (l, 0, j)),
        ],
        out_specs=pl.BlockSpec((None, 8, MOD_TN), lambda l, j: (l, 0, j)),
        out_shape=jax.ShapeDtypeStruct((depth, 8, n), F32),
        compiler_params=_cparams("arbitrary", "arbitrary"),
    )(cc, ada_w, ada_b.reshape(depth, 1, n))


def _mod_spec(layer, rows_per_mod, mod_base, tm, d):
    tiles_per_mod = rows_per_mod // tm
    return pl.BlockSpec((None, None, N_MOD, d), lambda t, *_: (layer, mod_base + t // tiles_per_mod, 0, 0))


def _ffn_kernel(*refs, mod_off, g_row, final, n_conv):
    refs = list(refs)
    x_ref, mod_ref, g_ref, wg_ref, wu_ref, wd_ref, wgt_ref, wut_ref, wdt_ref = refs[:9]
    pos = 9
    fg_ref = refs[pos] if final else None
    pos += int(final)
    conv_src = refs[pos:pos + n_conv]
    pos += n_conv
    o_ref = refs[pos]
    conv_dst = refs[pos + 1:pos + 1 + n_conv]
    h_ref = refs[pos + 1 + n_conv]
    j = pl.program_id(1)
    n_main = pl.num_programs(1) - 1

    for src, dst in zip(conv_src, conv_dst):
        dst[...] = src[...].astype(dst.dtype)

    tm, d = o_ref.shape
    row_tiles = [slice(r0, r0 + FFN_ROWS) for r0 in range(0, tm, FFN_ROWS)]

    def hidden(rows, wg, wu):
        h = h_ref[rows, :]
        gate = jnp.dot(h, wg, preferred_element_type=F32)
        up = jnp.dot(h, wu, preferred_element_type=F32)
        return (_silu(gate) * up).astype(BF16)

    @pl.when(j == 0)
    def _():
        _ada_norm_to(x_ref, h_ref, g_ref[g_row:g_row + 1, :],
                     mod_ref[mod_off:mod_off + 1, :], mod_ref[mod_off + 1:mod_off + 2, :])
        for rows in row_tiles:
            o_ref[rows, :] = jnp.dot(hidden(rows, wg_ref[...], wu_ref[...]), wd_ref[...],
                                     preferred_element_type=F32)

    @pl.when((j > 0) & (j < n_main))
    def _():
        for rows in row_tiles:
            o_ref[rows, :] += jnp.dot(hidden(rows, wg_ref[...], wu_ref[...]), wd_ref[...],
                                      preferred_element_type=F32)

    @pl.when(j == n_main)
    def _():
        for rows in row_tiles:
            a = hidden(rows, wgt_ref[...], wut_ref[...])
            ssq = None
            for c0 in range(0, d, EPI_COLS):
                cols = slice(c0, c0 + EPI_COLS)
                acc = o_ref[rows, cols] + jnp.dot(a, wdt_ref[:, cols], preferred_element_type=F32)
                o = x_ref[rows, cols] + (0.5 * mod_ref[mod_off + 2:mod_off + 3, cols]) * acc
                o_ref[rows, cols] = o
                if final:
                    part = jnp.sum(o * o, axis=-1, keepdims=True)
                    ssq = part if ssq is None else ssq + part
            if final:
                r = lax.rsqrt(ssq / d + EPS)
                for c0 in range(0, d, EPI_COLS):
                    cols = slice(c0, c0 + EPI_COLS)
                    o_ref[rows, cols] = o_ref[rows, cols] * r * fg_ref[:, cols]


def _ffn(x, mod, norm_g, w_gu, w_down, *, layer, which, rows_per_mod, mod_base, final_g=None, convert=()):
    t, d = x.shape
    f = w_down.shape[0]
    fc = FFN_FC
    n_main = (f - 1) // fc
    ft = f - n_main * fc
    assert ft % 128 == 0 and n_main >= 1
    tm = min(FFN_TM, t)
    final = final_g is not None
    el = pl.Element

    def main(j):
        return jnp.minimum(j, n_main - 1) * fc

    def wspec(rows, cols, off, bufs):
        return pl.BlockSpec((el(rows), el(cols)), lambda i, j: off(j), pipeline_mode=pl.Buffered(bufs))

    n_steps = (t // tm) * (n_main + 1)
    conv_in, conv_out, conv_shapes, conv_args = [], [], [], []
    for src, idx in convert:
        rows, cols = src.shape[-2:]
        slab = CONV_ROWS * -(-rows // (CONV_ROWS * n_steps))
        assert rows % slab == 0
        n_slabs = rows // slab

        def slab_idx(i, j, n_slabs=n_slabs):
            return jnp.minimum(i * (n_main + 1) + j, n_slabs - 1)

        lead = (None,) * len(idx)
        conv_in.append(pl.BlockSpec(lead + (slab, cols), lambda i, j, idx=idx, s=slab_idx: idx + (s(i, j), 0)))
        conv_out.append(pl.BlockSpec((slab, cols), lambda i, j, s=slab_idx: (s(i, j), 0)))
        conv_shapes.append(jax.ShapeDtypeStruct((rows, cols), BF16))
        conv_args.append(src)

    in_specs = [
        pl.BlockSpec((tm, d), lambda i, j: (i, 0)),
        _mod_spec(layer, rows_per_mod, mod_base, tm, d),
        pl.BlockSpec((None, 3, d), lambda i, j: (layer, 0, 0)),
        wspec(d, fc, lambda j: (0, main(j)), FFN_WBUFS),
        wspec(d, fc, lambda j: (0, pl.multiple_of(f + main(j), 128)), FFN_WBUFS),
        wspec(fc, d, lambda j: (main(j), 0), FFN_WBUFS),
        wspec(d, ft, lambda j: (0, n_main * fc), 1),
        wspec(d, ft, lambda j: (0, f + n_main * fc), 1),
        wspec(ft, d, lambda j: (n_main * fc, 0), 1),
    ]
    args = [x, mod, norm_g, w_gu, w_gu, w_down, w_gu, w_gu, w_down]
    if final:
        in_specs.append(pl.BlockSpec((1, d), lambda i, j: (0, 0)))
        args.append(final_g.reshape(1, d))
    outs = pl.pallas_call(
        functools.partial(_ffn_kernel, mod_off=6 * which, g_row=2 * which, final=final, n_conv=len(convert)),
        grid=(t // tm, n_main + 1),
        in_specs=in_specs + conv_in,
        out_specs=[pl.BlockSpec((tm, d), lambda i, j: (i, 0))] + conv_out,
        out_shape=[jax.ShapeDtypeStruct((t, d), F32)] + conv_shapes,
        scratch_shapes=[pltpu.VMEM((tm, d), BF16)],
        compiler_params=_cparams("arbitrary", "arbitrary"),
    )(*args, *conv_args)
    return outs[0], tuple(outs[1:])


def _inproj_kernel(x_ref, mod_ref, g_ref, w_ref, cs_ref, o_ref, h_ref):
    @pl.when(pl.program_id(1) == 0)
    def _():
        _ada_norm_to(x_ref, h_ref, g_ref[1:2, :], mod_ref[3:4, :], mod_ref[4:5, :])

    for r in range(0, o_ref.shape[0], PROJ_ROWS):
        acc = jnp.dot(h_ref[r:r + PROJ_ROWS, :], w_ref[...], preferred_element_type=F32)
        o_ref[r:r + PROJ_ROWS, :] = (acc * cs_ref[...]).astype(o_ref.dtype)


def _inproj(x, mod, norm_g, w, col_scale, *, layer, rows_per_mod, mod_base):
    t, d = x.shape
    n = w.shape[1]
    tm = min(PROJ_TM, t)
    return pl.pallas_call(
        _inproj_kernel,
        grid=(t // tm, n // PROJ_TN),
        in_specs=[
            pl.BlockSpec((tm, d), lambda i, j: (i, 0), pipeline_mode=pl.Buffered(1)),
            _mod_spec(layer, rows_per_mod, mod_base, tm, d),
            pl.BlockSpec((None, 3, d), lambda i, j: (layer, 0, 0)),
            pl.BlockSpec((d, PROJ_TN), lambda i, j: (0, j)),
            pl.BlockSpec((1, PROJ_TN), lambda i, j: (0, j)),
        ],
        out_specs=pl.BlockSpec((tm, PROJ_TN), lambda i, j: (i, j)),
        out_shape=jax.ShapeDtypeStruct((t, n), BF16),
        scratch_shapes=[pltpu.VMEM((tm, d), BF16)],
        compiler_params=_cparams("parallel", "arbitrary"),
    )(x, mod, norm_g, w, col_scale)


def _outproj_kernel(y_ref, w_ref, x_ref, mod_ref, o_ref):
    acc = jnp.dot(y_ref[...], w_ref[...], preferred_element_type=F32)
    o_ref[...] = x_ref[...] + mod_ref[5:6, :] * acc


def _outproj(y, w, x, mod, *, layer, rows_per_mod, mod_base):
    t, d = x.shape
    kd = y.shape[1]
    tm = min(OUT_TM, t)
    return pl.pallas_call(
        _outproj_kernel,
        grid=(t // tm,),
        in_specs=[
            pl.BlockSpec((tm, kd), lambda i: (i, 0)),
            pl.BlockSpec((kd, d), lambda i: (0, 0), pipeline_mode=pl.Buffered(1)),
            pl.BlockSpec((tm, d), lambda i: (i, 0)),
            _mod_spec(layer, rows_per_mod, mod_base, tm, d),
        ],
        out_specs=pl.BlockSpec((tm, d), lambda i: (i, 0)),
        out_shape=jax.ShapeDtypeStruct((t, d), F32),
        compiler_params=_cparams("parallel"),
    )(y, w, x, mod)


def _dot_nt(a, b):
    return lax.dot_general(a, b, (((1,), (1,)), ((), ())), preferred_element_type=F32)


def _dot_tn(a, b):
    return lax.dot_general(a, b, (((0,), (0,)), ((), ())), preferred_element_type=F32)


def _ret_consts(ld_ref):
    c = RET_CHUNK
    lg_f = -jnp.exp(ld_ref[0])[:, :1]
    lg_b = -jnp.exp(ld_ref[1])[:, :1]
    row = lax.broadcasted_iota(jnp.int32, (c, c), 0)
    col = lax.broadcasted_iota(jnp.int32, (c, c), 1)
    diff = (row - col).astype(F32)
    m_f = jnp.where(diff >= 0, jnp.exp(jnp.maximum(diff, 0.0) * lg_f), 0.0)
    m_b = jnp.where(diff <= 0, jnp.exp(jnp.maximum(-diff, 0.0) * lg_b), 0.0)
    pos = lax.broadcasted_iota(jnp.int32, (c, 1), 0).astype(F32)
    return dict(
        intra=m_f + m_b,
        qd_f=jnp.exp((pos + 1.0) * lg_f),
        qd_b=jnp.exp((c - pos) * lg_b),
        kd_f=jnp.exp((c - 1.0 - pos) * lg_f),
        kd_b=jnp.exp(pos * lg_b),
        cd_f=jnp.exp(float(c) * lg_f),
        cd_b=jnp.exp(float(c) * lg_b),
        lg_f=lg_f, lg_b=lg_b,
    )


def _ret_chunk_out(q, k, v, intra, inter):
    qb = q.astype(BF16)
    s = _dot_nt(qb, k.astype(BF16)) * intra
    o = jnp.dot(s.astype(BF16), v, preferred_element_type=F32)
    for state_bf16, qd in inter:
        o = o + jnp.dot(qb, state_bf16, preferred_element_type=F32) * qd
    return o


def _ret_gate(o, g):
    mu = jnp.mean(o, axis=-1, keepdims=True)
    var = jnp.mean(jnp.square(o - mu), axis=-1, keepdims=True)
    on = (o - mu) * lax.rsqrt(var + EPS)
    return (on * _silu(g.astype(F32))).astype(BF16)


def _ret_ctx_kernel(ld_ref, q_ref, k_ref, v_ref, g_ref, y_ref, st_ref):
    c = RET_CHUNK
    n = q_ref.shape[0]
    nc = n // c
    cs = _ret_consts(ld_ref)
    k_all = k_ref[...].astype(F32)
    v_all = v_ref[...]
    pos = lax.broadcasted_iota(jnp.int32, (n, 1), 0).astype(F32)
    w_f = jnp.exp((n - 1.0 - pos) * cs["lg_f"])
    w_b = jnp.exp(pos * cs["lg_b"])
    st_ref[0] = _dot_tn((k_all * w_f).astype(BF16), v_all)
    st_ref[1] = _dot_tn((k_all * w_b).astype(BF16), v_all)

    kv_f, kv_b = [], []
    for i in range(nc):
        kc = k_all[i * c:(i + 1) * c]
        vc = v_all[i * c:(i + 1) * c]
        kv_f.append(_dot_tn((kc * cs["kd_f"]).astype(BF16), vc) if i < nc - 1 else None)
        kv_b.append(_dot_tn((kc * cs["kd_b"]).astype(BF16), vc) if i > 0 else None)
    for i in range(nc):
        s_f = None
        for jj in range(i):
            s_f = kv_f[jj] if s_f is None else s_f * cs["cd_f"] + kv_f[jj]
        s_b = None
        for jj in range(nc - 1, i, -1):
            s_b = kv_b[jj] if s_b is None else s_b * cs["cd_b"] + kv_b[jj]
        inter = []
        if s_f is not None:
            inter.append((s_f.astype(BF16), cs["qd_f"]))
        if s_b is not None:
            inter.append((s_b.astype(BF16), cs["qd_b"]))
        sl = slice(i * c, (i + 1) * c)
        o = _ret_chunk_out(q_ref[sl, :].astype(F32), k_all[sl], v_all[sl], cs["intra"], inter)
        y_ref[sl, :] = _ret_gate(o, g_ref[sl, :])


def _rope_chunk(x, cos_r, sin_r, cos_c, sin_c):
    half = x.shape[1] // 2
    xr, xc = x[:, :half], x[:, half:]
    rr = xr * cos_r + pltpu.roll(xr, half // 2, axis=1) * sin_r
    rc = xc * cos_c + pltpu.roll(xc, half // 2, axis=1) * sin_c
    return jnp.concatenate([rr, rc], axis=1)


def _ret_lat_kernel(ld_ref, st_ref, rr_ref, rc_ref, q_ref, k_ref, v_ref, g_ref, y_ref, sb_ref, s_ref):
    c = RET_CHUNK
    rows_per_chunk = c // GRID_W
    nc = q_ref.shape[0] // c
    half = rr_ref.shape[2]
    cs = _ret_consts(ld_ref)
    cos_c, sin_c = rc_ref[0], rc_ref[1]

    def rope_tables(i):
        r0 = i * rows_per_chunk

        def rows(t):
            return jnp.concatenate(
                [jnp.broadcast_to(rr_ref[t, pl.ds(r0 + r, 1), :], (GRID_W, half)) for r in range(rows_per_chunk)], axis=0)

        return rows(0), rows(1)

    def chunk(ref, i):
        return ref[pl.ds(pl.multiple_of(i * c, c), c), :]

    s_ref[...] = st_ref[1]

    def back(t, carry):
        i = nc - 1 - t
        sb_ref[i] = s_ref[...].astype(BF16)
        cos_r, sin_r = rope_tables(i)
        k = _rope_chunk(chunk(k_ref, i).astype(F32), cos_r, sin_r, cos_c, sin_c)
        s_ref[...] = s_ref[...] * cs["cd_b"] + _dot_tn((k * cs["kd_b"]).astype(BF16), chunk(v_ref, i))
        return carry

    lax.fori_loop(0, nc, back, 0, unroll=2)
    s_ref[...] = st_ref[0]

    def fwd(i, carry):
        cos_r, sin_r = rope_tables(i)
        q = _rope_chunk(chunk(q_ref, i).astype(F32), cos_r, sin_r, cos_c, sin_c)
        k = _rope_chunk(chunk(k_ref, i).astype(F32), cos_r, sin_r, cos_c, sin_c)
        v = chunk(v_ref, i)
        inter = [(s_ref[...].astype(BF16), cs["qd_f"]), (sb_ref[i], cs["qd_b"])]
        o = _ret_chunk_out(q, k, v, cs["intra"], inter)
        y_ref[pl.ds(pl.multiple_of(i * c, c), c), :] = _ret_gate(o, chunk(g_ref, i))
        s_ref[...] = s_ref[...] * cs["cd_f"] + _dot_tn((k * cs["kd_f"]).astype(BF16), v)
        return carry

    lax.fori_loop(0, nc, fwd, 0, unroll=2)


def _rope_tables(n_tok, dim):
    half = dim // 2
    freqs = ROPE_BASE ** (-jnp.arange(0, half, 2, dtype=F32) / half)

    def tables(pos):
        ang = pos[:, None] * freqs
        cos, sin = jnp.cos(ang), jnp.sin(ang)
        return jnp.stack([jnp.concatenate([cos, cos], axis=-1), jnp.concatenate([-sin, sin], axis=-1)])

    rows = jnp.arange(n_tok // GRID_W).astype(F32)
    cols = (jnp.arange(RET_CHUNK) % GRID_W).astype(F32)
    return tables(rows), tables(cols)


def _retention(qkvg_lat, qkvg_ctx, ld, rope_r, rope_c, *, batch):
    t_lat, width = qkvg_lat.shape
    t_ctx = qkvg_ctx.shape[0]
    n, lc = t_lat // batch, t_ctx // batch
    h = RET_HEADS
    vw = width // 3
    dk, dv = (width - 2 * vw) // (2 * h), vw // h
    kb, vb, gb = h, (2 * h * dk) // dv, (2 * h * dk) // dv + h
    assert n % RET_CHUNK == 0 and lc % RET_CHUNK == 0 and RET_CHUNK % GRID_W == 0
    nc = n // RET_CHUNK
    ld_b = jnp.broadcast_to(ld.astype(F32)[:, :, None, None], (2, h, 1, 128))
    ld_spec = pl.BlockSpec((2, None, 1, 128), lambda b, hh: (0, hh, 0, 0))

    y_ctx, states = pl.pallas_call(
        _ret_ctx_kernel,
        grid=(batch, h),
        in_specs=[
            ld_spec,
            pl.BlockSpec((lc, dk), lambda b, hh: (b, hh)),
            pl.BlockSpec((lc, dk), lambda b, hh: (b, kb + hh)),
            pl.BlockSpec((lc, dv), lambda b, hh: (b, vb + hh)),
            pl.BlockSpec((lc, dv), lambda b, hh: (b, gb + hh)),
        ],
        out_specs=[
            pl.BlockSpec((lc, dv), lambda b, hh: (b, hh)),
            pl.BlockSpec((None, None, 2, dk, dv), lambda b, hh: (b, hh, 0, 0, 0)),
        ],
        out_shape=[
            jax.ShapeDtypeStruct((t_ctx, h * dv), BF16),
            jax.ShapeDtypeStruct((batch, h, 2, dk, dv), F32),
        ],
        compiler_params=_cparams("parallel", "parallel"),
    )(ld_b, qkvg_ctx, qkvg_ctx, qkvg_ctx, qkvg_ctx)

    y_lat = pl.pallas_call(
        _ret_lat_kernel,
        grid=(batch, h),
        in_specs=[
            ld_spec,
            pl.BlockSpec((None, None, 2, dk, dv), lambda b, hh: (b, hh, 0, 0, 0)),
            pl.BlockSpec(rope_r.shape, lambda b, hh: (0, 0, 0)),
            pl.BlockSpec(rope_c.shape, lambda b, hh: (0, 0, 0)),
            pl.BlockSpec((n, dk), lambda b, hh: (b, hh)),
            pl.BlockSpec((n, dk), lambda b, hh: (b, kb + hh)),
            pl.BlockSpec((n, dv), lambda b, hh: (b, vb + hh)),
            pl.BlockSpec((n, dv), lambda b, hh: (b, gb + hh)),
        ],
        out_specs=pl.BlockSpec((n, dv), lambda b, hh: (b, hh)),
        out_shape=jax.ShapeDtypeStruct((t_lat, h * dv), BF16),
        scratch_shapes=[pltpu.VMEM((nc, dk, dv), BF16), pltpu.VMEM((dk, dv), F32)],
        compiler_params=_cparams("parallel", "parallel"),
    )(ld_b, states, rope_r, rope_c, qkvg_lat, qkvg_lat, qkvg_lat, qkvg_lat)
    return y_lat, y_ctx


NA_SPAN = NA_QROWS + NA_KR


def _na_bias(rpb, rows):
    w, span = GRID_W, NA_SPAN
    col = np.arange(w)
    cstart = np.clip(col - NA_KC // 2, 0, w - NA_KC)
    c_ok = (col[None, :] >= cstart[:, None]) & (col[None, :] < cstart[:, None] + NA_KC)
    p = jnp.pad(rpb.astype(F32), ((0, 0), (0, 0), (w - 1, w - 1)), constant_values=NEG_BIG)
    off = NA_KC - 1 + w - 1
    tiles = jnp.stack([p[:, :, off - c:off - c + w] for c in range(w)], axis=2)
    tiles = jnp.where(c_ok[None, None], tiles, NEG_BIG)
    masked = jnp.full((rpb.shape[0], w, w), NEG_BIG, F32)
    variants = [(0, 0), (NA_QROWS, 0), (rows - NA_QROWS, rows - span)]
    out = []
    for r0, ks in variants:
        q_rows = []
        for ri in range(NA_QROWS):
            r = r0 + ri
            rs = min(max(r - NA_KR // 2, 0), rows - NA_KR)
            row_tiles = []
            for kr in range(span):
                krow = ks + kr
                row_tiles.append(tiles[:, krow - r + NA_KR - 1] if rs <= krow < rs + NA_KR else masked)
            q_rows.append(jnp.concatenate(row_tiles, axis=-1))
        out.append(jnp.concatenate(q_rows, axis=1))
    return jnp.stack(out, axis=1)


def _softmax_pv(parts):
    m = None
    for s, _ in parts:
        mi = jnp.max(s, axis=-1, keepdims=True)
        m = mi if m is None else jnp.maximum(m, mi)
    num, den = None, None
    for s, v in parts:
        p = jnp.exp(s - m)
        di = jnp.sum(p, axis=-1, keepdims=True)
        ni = jnp.dot(p.astype(BF16), v, preferred_element_type=F32)
        num = ni if num is None else num + ni
        den = di if den is None else den + di
    return num / den


def _na_kernel(bias_ref, q_ref, k_ref, v_ref, qc_ref, kc_ref, vc_ref, o_ref, oc_ref, *, rows, need_ctx):
    nblk = rows // NA_QROWS
    qn, sn = NA_QROWS * GRID_W, NA_SPAN * GRID_W
    kc, vc = kc_ref[...], vc_ref[...]

    if need_ctx:
        oc = _softmax_pv([(_dot_nt(qc_ref[...], kc), vc)])
        oc_ref[...] = oc.astype(oc_ref.dtype)
    else:
        oc_ref[...] = jnp.zeros_like(oc_ref)

    def block(i, carry):
        r0 = i * NA_QROWS
        ks = jnp.minimum(jnp.maximum(r0 - NA_KR // 2, 0), rows - NA_SPAN)
        variant = jnp.where(i == 0, 0, jnp.where(i == nblk - 1, 2, 1))
        q = q_ref[pl.ds(pl.multiple_of(i * qn, qn), qn), :]
        k0 = pl.multiple_of(ks * GRID_W, NA_QROWS * GRID_W)
        k = k_ref[pl.ds(k0, sn), :]
        v = v_ref[pl.ds(k0, sn), :]
        s_loc = _dot_nt(q, k) + bias_ref[variant]
        s_ctx = _dot_nt(q, kc)
        o = _softmax_pv([(s_loc, v), (s_ctx, vc)])
        o_ref[pl.ds(pl.multiple_of(i * qn, qn), qn), :] = o.astype(o_ref.dtype)
        return carry

    lax.fori_loop(0, nblk, block, 0, unroll=2)


def _na_attention(qkv_lat, qkv_ctx, bias, *, batch, need_ctx):
    t_lat, width = qkv_lat.shape
    t_ctx = qkv_ctx.shape[0]
    n, lc = t_lat // batch, t_ctx // batch
    h = NA_HEADS
    dh = width // (3 * h)
    rows = n // GRID_W
    assert rows % NA_QROWS == 0 and (NA_KR // 2) % NA_QROWS == 0 and rows >= NA_SPAN
    return pl.pallas_call(
        functools.partial(_na_kernel, rows=rows, need_ctx=need_ctx),
        grid=(batch, h),
        in_specs=[
            pl.BlockSpec((None,) + bias.shape[1:], lambda b, hh: (hh, 0, 0, 0)),
            pl.BlockSpec((n, dh), lambda b, hh: (b, hh)),
            pl.BlockSpec((n, dh), lambda b, hh: (b, h + hh)),
            pl.BlockSpec((n, dh), lambda b, hh: (b, 2 * h + hh)),
            pl.BlockSpec((lc, dh), lambda b, hh: (b, hh)),
            pl.BlockSpec((lc, dh), lambda b, hh: (b, h + hh)),
            pl.BlockSpec((lc, dh), lambda b, hh: (b, 2 * h + hh)),
        ],
        out_specs=[
            pl.BlockSpec((n, dh), lambda b, hh: (b, hh)),
            pl.BlockSpec((lc, dh), lambda b, hh: (b, hh)),
        ],
        out_shape=[
            jax.ShapeDtypeStruct((t_lat, h * dh), BF16),
            jax.ShapeDtypeStruct((t_ctx, h * dh), BF16),
        ],
        compiler_params=_cparams("parallel", "parallel"),
    )(bias, qkv_lat, qkv_lat, qkv_lat, qkv_ctx, qkv_ctx, qkv_ctx)


def kernel(x, c, ctx, c_ctx, ada_w, ada_b, norm_g, ffn_w_gu, ffn_w_down, ret_w_in, ret_w_out,
           ret_log_decay, na_w_in, na_w_out, na_rpb, final_g):
    batch, n_lat, d = x.shape
    l_ctx = ctx.shape[1]
    depth = ada_w.shape[0]
    n_mixers = 2
    assert n_lat % GRID_W == 0 and n_lat % PROJ_TM == 0 and (batch * l_ctx) % FFN_TM == 0

    xt = x.reshape(batch * n_lat, d)
    ct = ctx.reshape(batch * l_ctx, d)

    cc = jnp.zeros((8, d), F32).at[:batch].set(c).at[batch].set(c_ctx)
    mod = _modulation(cc, ada_w, ada_b).reshape(depth, 8, N_MOD, d)
    lat = dict(rows_per_mod=n_lat, mod_base=0)
    cx = dict(rows_per_mod=batch * l_ctx, mod_base=batch)

    ret_dk = d // RET_HEADS
    rope_r, rope_c = _rope_tables(n_lat, ret_dk)
    ret_w = ret_w_in.shape[2]
    ret_scale = jnp.ones((1, ret_w), F32).at[:, d:2 * d].set(ret_dk ** -0.5)
    na_dh = d // NA_HEADS
    na_scale = jnp.ones((1, 3 * d), F32).at[:, :d].set(na_dh ** -0.5)

    w_gu, w_dn = ffn_w_gu[0, 0].astype(BF16), ffn_w_down[0, 0].astype(BF16)

    for i in range(depth):
        last = i == depth - 1
        j = i // n_mixers
        ret_layer = i % n_mixers == 0
        mix_in, mix_out = (ret_w_in, ret_w_out) if ret_layer else (na_w_in, na_w_out)

        conv = [(ffn_w_gu, (i, 1)), (ffn_w_down, (i, 1)), (mix_in, (j,)), (mix_out, (j,))]
        xt, (w_gu2, w_dn2, w_in, w_out) = _ffn(xt, mod, norm_g, w_gu, w_dn, layer=i, which=0, convert=conv, **lat)
        ct, _ = _ffn(ct, mod, norm_g, w_gu, w_dn, layer=i, which=0, **cx)

        if ret_layer:
            p_lat = _inproj(xt, mod, norm_g, w_in, ret_scale, layer=i, **lat)
            p_ctx = _inproj(ct, mod, norm_g, w_in, ret_scale, layer=i, **cx)
            y_lat, y_ctx = _retention(p_lat, p_ctx, ret_log_decay[j], rope_r, rope_c, batch=batch)
        else:
            p_lat = _inproj(xt, mod, norm_g, w_in, na_scale, layer=i, **lat)
            p_ctx = _inproj(ct, mod, norm_g, w_in, na_scale, layer=i, **cx)
            bias = _na_bias(na_rpb[j], n_lat // GRID_W)
            y_lat, y_ctx = _na_attention(p_lat, p_ctx, bias, batch=batch, need_ctx=not last)

        xt = _outproj(y_lat, w_out, xt, mod, layer=i, **lat)
        conv = [] if last else [(ffn_w_gu, (i + 1, 0)), (ffn_w_down, (i + 1, 0))]
        xt, nxt = _ffn(xt, mod, norm_g, w_gu2, w_dn2, layer=i, which=1, final_g=final_g if last else None,
                       convert=conv, **lat)
        if not last:
            ct = _outproj(y_ctx, w_out, ct, mod, layer=i, **cx)
            ct, _ = _ffn(ct, mod, norm_g, w_gu2, w_dn2, layer=i, which=1, **cx)
            w_gu, w_dn = nxt

    return xt.reshape(batch, n_lat, d)
```

```python
import functools

import numpy as np
import jax
import jax.numpy as jnp
from jax import lax
from jax.experimental import pallas as pl
from jax.experimental.pallas import tpu as pltpu

F32 = jnp.float32
BF16 = jnp.bfloat16

GRID_W = 64
N_MOD = 9
RET_HEADS = 8
NA_HEADS = 16
NA_KR = 8
NA_KC = 16
ROPE_BASE = 10000.0
EPS = 1e-6

V7X_VMEM_BYTES = 64 * 1024 * 1024
VMEM_LIMIT_BYTES = V7X_VMEM_BYTES - 8 * 1024 * 1024
NEG_BIG = -1e30

RET_CHUNK = 256
FFN_TM = 512
FFN_ROWS = 512
FFN_FC = 512
FFN_WBUFS = 2
CONV_ROWS = 16
PROJ_TM = 1024
PROJ_TN = 1024
OUT_TM = 512
MOD_TN = 1024
NORM_ROWS = 16
NORM_UNROLL = 8
EPI_COLS = 512
PROJ_ROWS = 512
NA_QROWS = 4


def _cparams(*sem):
    return pltpu.CompilerParams(dimension_semantics=sem, vmem_limit_bytes=VMEM_LIMIT_BYTES)


def _silu(x):
    return x * (1.0 / (1.0 + jnp.exp(-x)))


def _ada_norm_tile(x, g, shift, scale):
    y = x * lax.rsqrt(jnp.mean(x * x, axis=-1, keepdims=True) + EPS) * g
    return y * (1.0 + scale) + shift


def _ada_norm_to(x_ref, h_ref, g, shift, scale):
    def body(r, carry):
        sl = pl.ds(pl.multiple_of(r * NORM_ROWS, NORM_ROWS), NORM_ROWS)
        h_ref[sl, :] = _ada_norm_tile(x_ref[sl, :], g, shift, scale).astype(BF16)
        return carry

    lax.fori_loop(0, x_ref.shape[0] // NORM_ROWS, body, 0, unroll=NORM_UNROLL)


def _mod_kernel(c_ref, w_ref, b_ref, o_ref):
    s = _silu(c_ref[...]).astype(BF16)
    o_ref[...] = jnp.dot(s, w_ref[...].astype(BF16), preferred_element_type=F32) + b_ref[...]


def _modulation(cc, ada_w, ada_b):
    depth, d, n = ada_w.shape
    return pl.pallas_call(
        _mod_kernel,
        grid=(depth, n // MOD_TN),
        in_specs=[
            pl.BlockSpec((8, d), lambda l, j: (0, 0)),
            pl.BlockSpec((None, d, MOD_TN), lambda l, j: (l, 0, j)),
            pl.BlockSpec((None, 1, MOD_TN), lambda l, j: (l, 0, j)),
        ],
        out_specs=pl.BlockSpec((None, 8, MOD_TN), lambda l, j: (l, 0, j)),
        out_shape=jax.ShapeDtypeStruct((depth, 8, n), F32),
        compiler_params=_cparams("arbitrary", "arbitrary"),
    )(cc, ada_w, ada_b.reshape(depth, 1, n))


def _mod_spec(layer, rows_per_mod, mod_base, tm, d):
    tiles_per_mod = rows_per_mod // tm
    return pl.BlockSpec((None, None, N_MOD, d), lambda t, *_: (layer, mod_base + t // tiles_per_mod, 0, 0))


def _ffn_kernel(*refs, mod_off, g_row, final, n_conv):
    refs = list(refs)
    x_ref, mod_ref, g_ref, wg_ref, wu_ref, wd_ref, wgt_ref, wut_ref, wdt_ref = refs[:9]
    pos = 9
    fg_ref = refs[pos] if final else None
    pos += int(final)
    conv_src = refs[pos:pos + n_conv]
    pos += n_conv
    o_ref = refs[pos]
    conv_dst = refs[pos + 1:pos + 1 + n_conv]
    h_ref = refs[pos + 1 + n_conv]
    j = pl.program_id(1)
    n_main = pl.num_programs(1) - 1

    for src, dst in zip(conv_src, conv_dst):
        dst[...] = src[...].astype(dst.dtype)

    tm, d = o_ref.shape
    row_tiles = [slice(r0, r0 + FFN_ROWS) for r0 in range(0, tm, FFN_ROWS)]

    def hidden(rows, wg, wu):
        h = h_ref[rows, :]
        gate = jnp.dot(h, wg, preferred_element_type=F32)
        up = jnp.dot(h, wu, preferred_element_type=F32)
        return (_silu(gate) * up).astype(BF16)

    @pl.when(j == 0)
    def _():
        _ada_norm_to(x_ref, h_ref, g_ref[g_row:g_row + 1, :],
                     mod_ref[mod_off:mod_off + 1, :], mod_ref[mod_off + 1:mod_off + 2, :])
        for rows in row_tiles:
            o_ref[rows, :] = jnp.dot(hidden(rows, wg_ref[...], wu_ref[...]), wd_ref[...],
                                     preferred_element_type=F32)

    @pl.when((j > 0) & (j < n_main))
    def _():
        for rows in row_tiles:
            o_ref[rows, :] += jnp.dot(hidden(rows, wg_ref[...], wu_ref[...]), wd_ref[...],
                                      preferred_element_type=F32)

    @pl.when(j == n_main)
    def _():
        for rows in row_tiles:
            a = hidden(rows, wgt_ref[...], wut_ref[...])
            ssq = None
            for c0 in range(0, d, EPI_COLS):
                cols = slice(c0, c0 + EPI_COLS)
                acc = o_ref[rows, cols] + jnp.dot(a, wdt_ref[:, cols], preferred_element_type=F32)
                o = x_ref[rows, cols] + (0.5 * mod_ref[mod_off + 2:mod_off + 3, cols]) * acc
                o_ref[rows, cols] = o
                if final:
                    part = jnp.sum(o * o, axis=-1, keepdims=True)
                    ssq = part if ssq is None else ssq + part
            if final:
                r = lax.rsqrt(ssq / d + EPS)
                for c0 in range(0, d, EPI_COLS):
                    cols = slice(c0, c0 + EPI_COLS)
                    o_ref[rows, cols] = o_ref[rows, cols] * r * fg_ref[:, cols]


def _ffn(x, mod, norm_g, w_gu, w_down, *, layer, which, rows_per_mod, mod_base, final_g=None, convert=()):
    t, d = x.shape
    f = w_down.shape[0]
    fc = FFN_FC
    n_main = (f - 1) // fc
    ft = f - n_main * fc
    assert ft % 128 == 0 and n_main >= 1
    tm = min(FFN_TM, t)
    final = final_g is not None
    el = pl.Element

    def main(j):
        return jnp.minimum(j, n_main - 1) * fc

    def wspec(rows, cols, off, bufs):
        return pl.BlockSpec((el(rows), el(cols)), lambda i, j: off(j), pipeline_mode=pl.Buffered(bufs))

    n_steps = (t // tm) * (n_main + 1)
    conv_in, conv_out, conv_shapes, conv_args = [], [], [], []
    for src, idx in convert:
        rows, cols = src.shape[-2:]
        slab = CONV_ROWS * -(-rows // (CONV_ROWS * n_steps))
        assert rows % slab == 0
        n_slabs = rows // slab

        def slab_idx(i, j, n_slabs=n_slabs):
            return jnp.minimum(i * (n_main + 1) + j, n_slabs - 1)

        lead = (None,) * len(idx)
        conv_in.append(pl.BlockSpec(lead + (slab, cols), lambda i, j, idx=idx, s=slab_idx: idx + (s(i, j), 0)))
        conv_out.append(pl.BlockSpec((slab, cols), lambda i, j, s=slab_idx: (s(i, j), 0)))
        conv_shapes.append(jax.ShapeDtypeStruct((rows, cols), BF16))
        conv_args.append(src)

    in_specs = [
        pl.BlockSpec((tm, d), lambda i, j: (i, 0)),
        _mod_spec(layer, rows_per_mod, mod_base, tm, d),
        pl.BlockSpec((None, 3, d), lambda i, j: (layer, 0, 0)),
        wspec(d, fc, lambda j: (0, main(j)), FFN_WBUFS),
        wspec(d, fc, lambda j: (0, pl.multiple_of(f + main(j), 128)), FFN_WBUFS),
        wspec(fc, d, lambda j: (main(j), 0), FFN_WBUFS),
        wspec(d, ft, lambda j: (0, n_main * fc), 1),
        wspec(d, ft, lambda j: (0, f + n_main * fc), 1),
        wspec(ft, d, lambda j: (n_main * fc, 0), 1),
    ]
    args = [x, mod, norm_g, w_gu, w_gu, w_down, w_gu, w_gu, w_down]
    if final:
        in_specs.append(pl.BlockSpec((1, d), lambda i, j: (0, 0)))
        args.append(final_g.reshape(1, d))
    outs = pl.pallas_call(
        functools.partial(_ffn_kernel, mod_off=6 * which, g_row=2 * which, final=final, n_conv=len(convert)),
        grid=(t // tm, n_main + 1),
        in_specs=in_specs + conv_in,
        out_specs=[pl.BlockSpec((tm, d), lambda i, j: (i, 0))] + conv_out,
        out_shape=[jax.ShapeDtypeStruct((t, d), F32)] + conv_shapes,
        scratch_shapes=[pltpu.VMEM((tm, d), BF16)],
        compiler_params=_cparams("arbitrary", "arbitrary"),
    )(*args, *conv_args)
    return outs[0], tuple(outs[1:])


def _rope_heads(x, cos, sin):
    dk = cos.shape[1]
    half = dk // 2
    out = []
    for c0 in range(0, x.shape[1], half):
        xs = x[:, c0:c0 + half]
        t0 = c0 % dk
        out.append(xs * cos[:, t0:t0 + half] + pltpu.roll(xs, half // 2, axis=1) * sin[:, t0:t0 + half])
    return jnp.concatenate(out, axis=1)


def _inproj_kernel(*refs, rope_tiles):
    if rope_tiles:
        x_ref, mod_ref, g_ref, w_ref, cs_ref, cos_ref, sin_ref, o_ref, h_ref = refs
    else:
        x_ref, mod_ref, g_ref, w_ref, cs_ref, o_ref, h_ref = refs
    j = pl.program_id(1)

    @pl.when(j == 0)
    def _():
        _ada_norm_to(x_ref, h_ref, g_ref[1:2, :], mod_ref[3:4, :], mod_ref[4:5, :])

    def project(rope):
        for r in range(0, o_ref.shape[0], PROJ_ROWS):
            rows = slice(r, r + PROJ_ROWS)
            acc = jnp.dot(h_ref[rows, :], w_ref[...], preferred_element_type=F32) * cs_ref[...]
            if rope:
                acc = _rope_heads(acc, cos_ref[rows, :], sin_ref[rows, :])
            o_ref[rows, :] = acc.astype(o_ref.dtype)

    if rope_tiles:
        pl.when(j < rope_tiles)(lambda: project(True))
        pl.when(j >= rope_tiles)(lambda: project(False))
    else:
        project(False)


def _inproj(x, mod, norm_g, w, col_scale, *, layer, rows_per_mod, mod_base, rope=None):
    t, d = x.shape
    n = w.shape[1]
    tm = min(PROJ_TM, t)
    in_specs = [
        pl.BlockSpec((tm, d), lambda i, j: (i, 0)),
        _mod_spec(layer, rows_per_mod, mod_base, tm, d),
        pl.BlockSpec((None, 3, d), lambda i, j: (layer, 0, 0)),
        pl.BlockSpec((d, PROJ_TN), lambda i, j: (0, j)),
        pl.BlockSpec((1, PROJ_TN), lambda i, j: (0, j)),
    ]
    args = [x, mod, norm_g, w, col_scale]
    rope_tiles = 0
    if rope is not None:
        cos, sin, width = rope
        seq_tiles = cos.shape[0] // tm
        assert cos.shape[0] % tm == 0 and width % PROJ_TN == 0 and PROJ_TN % cos.shape[1] == 0
        rope_tiles = width // PROJ_TN
        in_specs += [pl.BlockSpec((tm, cos.shape[1]), lambda i, j: (i % seq_tiles, 0))] * 2
        args += [cos, sin]
    return pl.pallas_call(
        functools.partial(_inproj_kernel, rope_tiles=rope_tiles),
        grid=(t // tm, n // PROJ_TN),
        in_specs=in_specs,
        out_specs=pl.BlockSpec((tm, PROJ_TN), lambda i, j: (i, j)),
        out_shape=jax.ShapeDtypeStruct((t, n), BF16),
        scratch_shapes=[pltpu.VMEM((tm, d), BF16)],
        compiler_params=_cparams("parallel", "arbitrary"),
    )(*args)


def _outproj_kernel(y_ref, w_ref, x_ref, mod_ref, o_ref):
    acc = jnp.dot(y_ref[...], w_ref[...], preferred_element_type=F32)
    o_ref[...] = x_ref[...] + mod_ref[5:6, :] * acc


def _outproj(y, w, x, mod, *, layer, rows_per_mod, mod_base):
    t, d = x.shape
    kd = y.shape[1]
    tm = min(OUT_TM, t)
    return pl.pallas_call(
        _outproj_kernel,
        grid=(t // tm,),
        in_specs=[
            pl.BlockSpec((tm, kd), lambda i: (i, 0)),
            pl.BlockSpec((kd, d), lambda i: (0, 0), pipeline_mode=pl.Buffered(1)),
            pl.BlockSpec((tm, d), lambda i: (i, 0)),
            _mod_spec(layer, rows_per_mod, mod_base, tm, d),
        ],
        out_specs=pl.BlockSpec((tm, d), lambda i: (i, 0)),
        out_shape=jax.ShapeDtypeStruct((t, d), F32),
        compiler_params=_cparams("parallel"),
    )(y, w, x, mod)


def _dot_nt(a, b):
    return lax.dot_general(a, b, (((1,), (1,)), ((), ())), preferred_element_type=F32)


def _dot_tn(a, b):
    return lax.dot_general(a, b, (((0,), (0,)), ((), ())), preferred_element_type=F32)


def _ret_consts(ld_ref):
    c = RET_CHUNK
    lg_f = -jnp.exp(ld_ref[0])[:, :1]
    lg_b = -jnp.exp(ld_ref[1])[:, :1]
    row = lax.broadcasted_iota(jnp.int32, (c, c), 0)
    col = lax.broadcasted_iota(jnp.int32, (c, c), 1)
    diff = (row - col).astype(F32)
    m_f = jnp.where(diff >= 0, jnp.exp(jnp.maximum(diff, 0.0) * lg_f), 0.0)
    m_b = jnp.where(diff <= 0, jnp.exp(jnp.maximum(-diff, 0.0) * lg_b), 0.0)
    pos = lax.broadcasted_iota(jnp.int32, (c, 1), 0).astype(F32)
    return dict(
        intra=m_f + m_b,
        qd_f=jnp.exp((pos + 1.0) * lg_f),
        qd_b=jnp.exp((c - pos) * lg_b),
        kd_f=jnp.exp((c - 1.0 - pos) * lg_f),
        kd_b=jnp.exp(pos * lg_b),
        cd_f=jnp.exp(float(c) * lg_f),
        cd_b=jnp.exp(float(c) * lg_b),
        lg_f=lg_f, lg_b=lg_b,
    )


def _ret_chunk_out(q, k, v, intra, inter):
    qb = q.astype(BF16)
    s = _dot_nt(qb, k.astype(BF16)) * intra
    o = jnp.dot(s.astype(BF16), v, preferred_element_type=F32)
    for state_bf16, qd in inter:
        o = o + jnp.dot(qb, state_bf16, preferred_element_type=F32) * qd
    return o


def _ret_gate(o, g):
    mu = jnp.mean(o, axis=-1, keepdims=True)
    var = jnp.mean(jnp.square(o - mu), axis=-1, keepdims=True)
    on = (o - mu) * lax.rsqrt(var + EPS)
    return (on * _silu(g.astype(F32))).astype(BF16)


def _ret_ctx_kernel(ld_ref, q_ref, k_ref, v_ref, g_ref, y_ref, st_ref):
    c = RET_CHUNK
    n = q_ref.shape[0]
    nc = n // c
    cs = _ret_consts(ld_ref)
    k_all = k_ref[...].astype(F32)
    v_all = v_ref[...]
    pos = lax.broadcasted_iota(jnp.int32, (n, 1), 0).astype(F32)
    w_f = jnp.exp((n - 1.0 - pos) * cs["lg_f"])
    w_b = jnp.exp(pos * cs["lg_b"])
    st_ref[0] = _dot_tn((k_all * w_f).astype(BF16), v_all)
    st_ref[1] = _dot_tn((k_all * w_b).astype(BF16), v_all)

    kv_f, kv_b = [], []
    for i in range(nc):
        kc = k_all[i * c:(i + 1) * c]
        vc = v_all[i * c:(i + 1) * c]
        kv_f.append(_dot_tn((kc * cs["kd_f"]).astype(BF16), vc) if i < nc - 1 else None)
        kv_b.append(_dot_tn((kc * cs["kd_b"]).astype(BF16), vc) if i > 0 else None)
    for i in range(nc):
        s_f = None
        for jj in range(i):
            s_f = kv_f[jj] if s_f is None else s_f * cs["cd_f"] + kv_f[jj]
        s_b = None
        for jj in range(nc - 1, i, -1):
            s_b = kv_b[jj] if s_b is None else s_b * cs["cd_b"] + kv_b[jj]
        inter = []
        if s_f is not None:
            inter.append((s_f.astype(BF16), cs["qd_f"]))
        if s_b is not None:
            inter.append((s_b.astype(BF16), cs["qd_b"]))
        sl = slice(i * c, (i + 1) * c)
        o = _ret_chunk_out(q_ref[sl, :].astype(F32), k_all[sl], v_all[sl], cs["intra"], inter)
        y_ref[sl, :] = _ret_gate(o, g_ref[sl, :])


def _ret_lat_kernel(ld_ref, st_ref, q_ref, k_ref, v_ref, g_ref, y_ref, sb_ref, s_ref):
    c = RET_CHUNK
    nc = q_ref.shape[0] // c
    cs = _ret_consts(ld_ref)

    def chunk(ref, i):
        return ref[pl.ds(pl.multiple_of(i * c, c), c), :]

    s_ref[...] = st_ref[1]

    def back(t, carry):
        i = nc - 1 - t
        sb_ref[i] = s_ref[...].astype(BF16)
        k = chunk(k_ref, i).astype(F32)
        s_ref[...] = s_ref[...] * cs["cd_b"] + _dot_tn((k * cs["kd_b"]).astype(BF16), chunk(v_ref, i))
        return carry

    lax.fori_loop(0, nc, back, 0, unroll=2)
    s_ref[...] = st_ref[0]

    def fwd(i, carry):
        q = chunk(q_ref, i)
        k = chunk(k_ref, i)
        v = chunk(v_ref, i)
        inter = [(s_ref[...].astype(BF16), cs["qd_f"]), (sb_ref[i], cs["qd_b"])]
        o = _ret_chunk_out(q, k, v, cs["intra"], inter)
        y_ref[pl.ds(pl.multiple_of(i * c, c), c), :] = _ret_gate(o, chunk(g_ref, i))
        s_ref[...] = s_ref[...] * cs["cd_f"] + _dot_tn((k.astype(F32) * cs["kd_f"]).astype(BF16), v)
        return carry

    lax.fori_loop(0, nc, fwd, 0, unroll=2)


def _rope_tables(n_tok, dim):
    half = dim // 2
    freqs = ROPE_BASE ** (-jnp.arange(0, half, 2, dtype=F32) / half)
    tok = jnp.arange(n_tok)
    ang_r = (tok // GRID_W).astype(F32)[:, None] * freqs
    ang_c = (tok % GRID_W).astype(F32)[:, None] * freqs
    cos = jnp.concatenate([jnp.cos(ang_r)] * 2 + [jnp.cos(ang_c)] * 2, axis=-1)
    sin = jnp.concatenate([-jnp.sin(ang_r), jnp.sin(ang_r), -jnp.sin(ang_c), jnp.sin(ang_c)], axis=-1)
    return cos, sin


def _retention(qkvg_lat, qkvg_ctx, ld, *, batch):
    t_lat, width = qkvg_lat.shape
    t_ctx = qkvg_ctx.shape[0]
    n, lc = t_lat // batch, t_ctx // batch
    h = RET_HEADS
    vw = width // 3
    dk, dv = (width - 2 * vw) // (2 * h), vw // h
    kb, vb, gb = h, (2 * h * dk) // dv, (2 * h * dk) // dv + h
    assert n % RET_CHUNK == 0 and lc % RET_CHUNK == 0 and RET_CHUNK % GRID_W == 0
    nc = n // RET_CHUNK
    ld_b = jnp.broadcast_to(ld.astype(F32)[:, :, None, None], (2, h, 1, 128))
    ld_spec = pl.BlockSpec((2, None, 1, 128), lambda b, hh: (0, hh, 0, 0))

    y_ctx, states = pl.pallas_call(
        _ret_ctx_kernel,
        grid=(batch, h),
        in_specs=[
            ld_spec,
            pl.BlockSpec((lc, dk), lambda b, hh: (b, hh)),
            pl.BlockSpec((lc, dk), lambda b, hh: (b, kb + hh)),
            pl.BlockSpec((lc, dv), lambda b, hh: (b, vb + hh)),
            pl.BlockSpec((lc, dv), lambda b, hh: (b, gb + hh)),
        ],
        out_specs=[
            pl.BlockSpec((lc, dv), lambda b, hh: (b, hh)),
            pl.BlockSpec((None, None, 2, dk, dv), lambda b, hh: (b, hh, 0, 0, 0)),
        ],
        out_shape=[
            jax.ShapeDtypeStruct((t_ctx, h * dv), BF16),
            jax.ShapeDtypeStruct((batch, h, 2, dk, dv), F32),
        ],
        compiler_params=_cparams("parallel", "parallel"),
    )(ld_b, qkvg_ctx, qkvg_ctx, qkvg_ctx, qkvg_ctx)

    y_lat = pl.pallas_call(
        _ret_lat_kernel,
        grid=(batch, h),
        in_specs=[
            ld_spec,
            pl.BlockSpec((None, None, 2, dk, dv), lambda b, hh: (b, hh, 0, 0, 0)),
            pl.BlockSpec((n, dk), lambda b, hh: (b, hh)),
            pl.BlockSpec((n, dk), lambda b, hh: (b, kb + hh)),
            pl.BlockSpec((n, dv), lambda b, hh: (b, vb + hh)),
            pl.BlockSpec((n, dv), lambda b, hh: (b, gb + hh)),
        ],
        out_specs=pl.BlockSpec((n, dv), lambda b, hh: (b, hh)),
        out_shape=jax.ShapeDtypeStruct((t_lat, h * dv), BF16),
        scratch_shapes=[pltpu.VMEM((nc, dk, dv), BF16), pltpu.VMEM((dk, dv), F32)],
        compiler_params=_cparams("parallel", "parallel"),
    )(ld_b, states, qkvg_lat, qkvg_lat, qkvg_lat, qkvg_lat)
    return y_lat, y_ctx


NA_SPAN = NA_QROWS + NA_KR


def _na_bias(rpb, rows):
    w, span = GRID_W, NA_SPAN
    col = np.arange(w)
    cstart = np.clip(col - NA_KC // 2, 0, w - NA_KC)
    c_ok = (col[None, :] >= cstart[:, None]) & (col[None, :] < cstart[:, None] + NA_KC)
    p = jnp.pad(rpb.astype(F32), ((0, 0), (0, 0), (w - 1, w - 1)), constant_values=NEG_BIG)
    off = NA_KC - 1 + w - 1
    tiles = jnp.stack([p[:, :, off - c:off - c + w] for c in range(w)], axis=2)
    tiles = jnp.where(c_ok[None, None], tiles, NEG_BIG)
    n_dr = tiles.shape[1]
    tiles = jnp.concatenate([tiles, jnp.full((rpb.shape[0], 1, w, w), NEG_BIG, F32)], axis=1)
    variants = [(0, 0), (NA_QROWS, 0), (rows - NA_QROWS, rows - span)]
    pick = np.full((len(variants), NA_QROWS, span), n_dr, np.int32)
    for vi, (r0, ks) in enumerate(variants):
        for ri in range(NA_QROWS):
            r = r0 + ri
            rs = min(max(r - NA_KR // 2, 0), rows - NA_KR)
            for kr in range(span):
                krow = ks + kr
                if rs <= krow < rs + NA_KR:
                    pick[vi, ri, kr] = krow - r + NA_KR - 1
    b = jnp.take(tiles, pick.reshape(-1), axis=1)
    b = b.reshape(rpb.shape[0], len(variants), NA_QROWS, span, w, w).transpose(0, 1, 2, 4, 3, 5)
    return b.reshape(rpb.shape[0], len(variants), NA_QROWS * w, span * w)


def _softmax_pv(parts):
    m = None
    for s, _ in parts:
        mi = jnp.max(s, axis=-1, keepdims=True)
        m = mi if m is None else jnp.maximum(m, mi)
    num, den = None, None
    for s, v in parts:
        p = jnp.exp(s - m)
        di = jnp.sum(p, axis=-1, keepdims=True)
        ni = jnp.dot(p.astype(BF16), v, preferred_element_type=F32)
        num = ni if num is None else num + ni
        den = di if den is None else den + di
    return num / den


def _na_kernel(bias_ref, q_ref, k_ref, v_ref, qc_ref, kc_ref, vc_ref, o_ref, oc_ref, s_ref, m_ref, *, rows, need_ctx):
    nblk = rows // NA_QROWS
    qn, sn = NA_QROWS * GRID_W, NA_SPAN * GRID_W
    kc, vc = kc_ref[...], vc_ref[...]

    if need_ctx:
        oc = _softmax_pv([(_dot_nt(qc_ref[...], kc), vc)])
        oc_ref[...] = oc.astype(oc_ref.dtype)
    else:
        oc_ref[...] = jnp.zeros_like(oc_ref)

    def slab_start(i):
        ks = jnp.minimum(jnp.maximum(i * NA_QROWS - NA_KR // 2, 0), rows - NA_SPAN)
        return pl.multiple_of(ks * GRID_W, NA_QROWS * GRID_W)

    def scores(i, slot):
        variant = jnp.where(i == 0, 0, jnp.where(i == nblk - 1, 2, 1))
        q = q_ref[pl.ds(pl.multiple_of(i * qn, qn), qn), :]
        s_loc = _dot_nt(q, k_ref[pl.ds(slab_start(i), sn), :]) + bias_ref[variant]
        s_ctx = _dot_nt(q, kc)
        s_ref[slot, :, :sn] = s_loc
        s_ref[slot, :, sn:] = s_ctx
        m = jnp.maximum(jnp.max(s_loc, axis=-1, keepdims=True), jnp.max(s_ctx, axis=-1, keepdims=True))
        m_ref[slot] = jnp.broadcast_to(m, m_ref.shape[1:])

    def attend(i, slot):
        m = m_ref[slot][:, :1]
        p_loc = jnp.exp(s_ref[slot, :, :sn] - m)
        p_ctx = jnp.exp(s_ref[slot, :, sn:] - m)
        den = jnp.sum(p_loc, axis=-1, keepdims=True) + jnp.sum(p_ctx, axis=-1, keepdims=True)
        num = (jnp.dot(p_loc.astype(BF16), v_ref[pl.ds(slab_start(i), sn), :], preferred_element_type=F32)
               + jnp.dot(p_ctx.astype(BF16), vc, preferred_element_type=F32))
        o_ref[pl.ds(pl.multiple_of(i * qn, qn), qn), :] = (num / den).astype(o_ref.dtype)

    scores(0, 0)

    def pair(t, carry):
        i = 2 * t
        scores(i + 1, 1)
        attend(i, 0)
        scores(jnp.minimum(i + 2, nblk - 1), 0)
        attend(i + 1, 1)
        return carry

    lax.fori_loop(0, nblk // 2, pair, 0)


def _na_attention(qkv_lat, qkv_ctx, bias, *, batch, need_ctx):
    t_lat, width = qkv_lat.shape
    t_ctx = qkv_ctx.shape[0]
    n, lc = t_lat // batch, t_ctx // batch
    h = NA_HEADS
    dh = width // (3 * h)
    rows = n // GRID_W
    assert rows % (2 * NA_QROWS) == 0 and (NA_KR // 2) % NA_QROWS == 0 and rows >= NA_SPAN
    qn, sn = NA_QROWS * GRID_W, NA_SPAN * GRID_W
    return pl.pallas_call(
        functools.partial(_na_kernel, rows=rows, need_ctx=need_ctx),
        grid=(batch, h),
        in_specs=[
            pl.BlockSpec((None,) + bias.shape[1:], lambda b, hh: (hh, 0, 0, 0)),
            pl.BlockSpec((n, dh), lambda b, hh: (b, hh)),
            pl.BlockSpec((n, dh), lambda b, hh: (b, h + hh)),
            pl.BlockSpec((n, dh), lambda b, hh: (b, 2 * h + hh)),
            pl.BlockSpec((lc, dh), lambda b, hh: (b, hh)),
            pl.BlockSpec((lc, dh), lambda b, hh: (b, h + hh)),
            pl.BlockSpec((lc, dh), lambda b, hh: (b, 2 * h + hh)),
        ],
        out_specs=[
            pl.BlockSpec((n, dh), lambda b, hh: (b, hh)),
            pl.BlockSpec((lc, dh), lambda b, hh: (b, hh)),
        ],
        out_shape=[
            jax.ShapeDtypeStruct((t_lat, h * dh), BF16),
            jax.ShapeDtypeStruct((t_ctx, h * dh), BF16),
        ],
        scratch_shapes=[pltpu.VMEM((2, qn, sn + lc), F32), pltpu.VMEM((2, qn, 128), F32)],
        compiler_params=_cparams("parallel", "parallel"),
    )(bias, qkv_lat, qkv_lat, qkv_lat, qkv_ctx, qkv_ctx, qkv_ctx)


def kernel(x, c, ctx, c_ctx, ada_w, ada_b, norm_g, ffn_w_gu, ffn_w_down, ret_w_in, ret_w_out,
           ret_log_decay, na_w_in, na_w_out, na_rpb, final_g):
    batch, n_lat, d = x.shape
    l_ctx = ctx.shape[1]
    depth = ada_w.shape[0]
    n_mixers = 2
    assert n_lat % GRID_W == 0 and n_lat % PROJ_TM == 0 and (batch * l_ctx) % FFN_TM == 0

    xt = x.reshape(batch * n_lat, d)
    ct = ctx.reshape(batch * l_ctx, d)

    cc = jnp.zeros((8, d), F32).at[:batch].set(c).at[batch].set(c_ctx)
    mod = _modulation(cc, ada_w, ada_b).reshape(depth, 8, N_MOD, d)
    lat = dict(rows_per_mod=n_lat, mod_base=0)
    cx = dict(rows_per_mod=batch * l_ctx, mod_base=batch)

    ret_dk = d // RET_HEADS
    rope_cos, rope_sin = _rope_tables(n_lat, ret_dk)
    rope = (rope_cos, rope_sin, 2 * d)
    ret_w = ret_w_in.shape[2]
    ret_scale = jnp.ones((1, ret_w), F32).at[:, d:2 * d].set(ret_dk ** -0.5)
    na_dh = d // NA_HEADS
    na_scale = jnp.ones((1, 3 * d), F32).at[:, :d].set(na_dh ** -0.5)

    w_gu, w_dn = ffn_w_gu[0, 0].astype(BF16), ffn_w_down[0, 0].astype(BF16)

    for i in range(depth):
        last = i == depth - 1
        j = i // n_mixers
        ret_layer = i % n_mixers == 0
        mix_in, mix_out = (ret_w_in, ret_w_out) if ret_layer else (na_w_in, na_w_out)

        conv = [(ffn_w_gu, (i, 1)), (ffn_w_down, (i, 1)), (mix_in, (j,)), (mix_out, (j,))]
        xt, (w_gu2, w_dn2, w_in, w_out) = _ffn(xt, mod, norm_g, w_gu, w_dn, layer=i, which=0, convert=conv, **lat)
        ct, _ = _ffn(ct, mod, norm_g, w_gu, w_dn, layer=i, which=0, **cx)

        if ret_layer:
            p_lat = _inproj(xt, mod, norm_g, w_in, ret_scale, layer=i, rope=rope, **lat)
            p_ctx = _inproj(ct, mod, norm_g, w_in, ret_scale, layer=i, **cx)
            y_lat, y_ctx = _retention(p_lat, p_ctx, ret_log_decay[j], batch=batch)
        else:
            p_lat = _inproj(xt, mod, norm_g, w_in, na_scale, layer=i, **lat)
            p_ctx = _inproj(ct, mod, norm_g, w_in, na_scale, layer=i, **cx)
            bias = _na_bias(na_rpb[j], n_lat // GRID_W)
            y_lat, y_ctx = _na_attention(p_lat, p_ctx, bias, batch=batch, need_ctx=not last)

        xt = _outproj(y_lat, w_out, xt, mod, layer=i, **lat)
        conv = [] if last else [(ffn_w_gu, (i + 1, 0)), (ffn_w_down, (i + 1, 0))]
        xt, nxt = _ffn(xt, mod, norm_g, w_gu2, w_dn2, layer=i, which=1, final_g=final_g if last else None,
                       convert=conv, **lat)
        if not last:
            ct = _outproj(y_ctx, w_out, ct, mod, layer=i, **cx)
            ct, _ = _ffn(ct, mod, norm_g, w_gu2, w_dn2, layer=i, which=1, **cx)
            w_gu, w_dn = nxt

    return xt.reshape(batch, n_lat, d)
```
